```python
import math
import jax, jax.numpy as jnp
from jax import lax
import numpy as np

D_MODEL = 4096
BATCH = 4
SEQ = 4096
DEPTH = 2

DN_ALPHA = (2 * DEPTH) ** 0.25
DN_BETA = (8 * DEPTH) ** -0.25
LN_EPS = 1e-5
RMS_EPS = 1e-6
NEG_INF = -1e30
FORCE_SCORE = 1e9

MIX_WIDTH = D_MODEL

A_HEADS = 4
A_DQK = 256
A_DV = 512
A_QK = A_HEADS * A_DQK
A_V = A_HEADS * A_DV
A_CHUNK = 64
A_GATE_CAP = 15.0

B_HEADS = 16
B_Q_LORA = 1024
B_KV_LORA = 512
B_NOPE = 128
B_ROPE = 64
B_DV = 128
ROPE_THETA = 10000.0
Q_BLOCK = 128

C_HEADS = 32
C_KV_GROUPS = 4
C_HD = 128
C_KV = C_KV_GROUPS * C_HD
C_CMP_LEN = 32
C_CMP_STRIDE = 16
C_SEL_LEN = 64
C_N_SEL = 16
C_WINDOW = 512
C_CMP_HIDDEN = 256
C_Q_BLOCK = 64

FFN_HIDDEN = -(-(8 * D_MODEL) // (3 * 256)) * 256

AB_SPLITS = [int(v) for v in np.cumsum([A_QK, A_QK, A_V, A_HEADS, A_HEADS, A_V, B_Q_LORA, B_KV_LORA])]
AB_IN = AB_SPLITS[-1] + B_ROPE
C_SPLITS = [int(v) for v in np.cumsum([C_HEADS * C_HD] + [C_KV] * 6)]
C_IN = C_SPLITS[-1] + 3 * C_HEADS

kernel_name = "hybrid_mlstm_mla_nsa_deepnorm"


def layer_norm(x, g, b):
    xf = x.astype(jnp.float32)
    mu = jnp.mean(xf, -1, keepdims=True)
    var = jnp.mean(jnp.square(xf - mu), -1, keepdims=True)
    return ((xf - mu) * lax.rsqrt(var + LN_EPS) * g + b).astype(x.dtype)


def rms_norm(x, g):
    xf = x.astype(jnp.float32)
    return (xf * lax.rsqrt(jnp.mean(xf * xf, -1, keepdims=True) + RMS_EPS) * g).astype(x.dtype)


def rope(x, pos):
    half = x.shape[-1] // 2
    inv = ROPE_THETA ** (-jnp.arange(half, dtype=jnp.float32) / half)
    ang = pos.astype(jnp.float32)[:, None] * inv[None, :]
    cos, sin = jnp.cos(ang)[:, None, :], jnp.sin(ang)[:, None, :]
    xf = x.astype(jnp.float32)
    x1, x2 = xf[..., :half], xf[..., half:]
    return jnp.concatenate([x1 * cos - x2 * sin, x1 * sin + x2 * cos], -1).astype(x.dtype)


def masked_softmax(s, mask, axis=-1):
    s = jnp.where(mask, s.astype(jnp.float32), NEG_INF)
    return jax.nn.softmax(s, axis=axis) * mask


def soft_cap(z):
    return A_GATE_CAP * jnp.tanh(z / A_GATE_CAP)


def mlstm_chunkwise(q, k, v, logi, logf):
    B, S, H, DQK = q.shape
    DV = v.shape[-1]
    nc = S // A_CHUNK

    def to_chunks(t):
        t = t.reshape((B, nc, A_CHUNK, H) + t.shape[3:])
        return jnp.moveaxis(t, (1, 3), (0, 2))

    causal = jnp.tril(jnp.ones((A_CHUNK, A_CHUNK), bool))

    def step(carry, inp):
        C, n, m = carry
        qc, kc, vc, li, lf = inp
        b = jnp.cumsum(lf, axis=-1)
        dmat = jnp.where(causal, b[..., :, None] - b[..., None, :] + li[..., None, :], -jnp.inf)
        m_inter = b + m[..., None]
        m_t = jnp.maximum(m_inter, jnp.max(dmat, -1))
        w_inter = jnp.exp(m_inter - m_t)
        s = jnp.einsum('bhtd,bhsd->bhts', qc, kc) * jnp.exp(dmat - m_t[..., None])
        num = w_inter[..., None] * jnp.einsum('bhtd,bhde->bhte', qc, C) + jnp.einsum('bhts,bhse->bhte', s, vc)
        den = w_inter * jnp.einsum('bhtd,bhd->bht', qc, n) + jnp.sum(s, -1)
        h = num / jnp.maximum(jnp.abs(den), jnp.exp(-m_t))[..., None]
        g = b[..., -1]
        w_s = g[..., None] - b + li
        m_new = jnp.maximum(g + m, jnp.max(w_s, -1))
        decay = jnp.exp(g + m - m_new)
        kw = kc * jnp.exp(w_s - m_new[..., None])[..., None]
        C_new = decay[..., None, None] * C + jnp.einsum('bhsd,bhse->bhde', kw, vc)
        n_new = decay[..., None] * n + jnp.sum(kw, axis=2)
        return (C_new, n_new, m_new), h

    f32 = jnp.float32
    init = (jnp.zeros((B, H, DQK, DV), f32), jnp.zeros((B, H, DQK), f32), jnp.zeros((B, H), f32))
    _, hs = lax.scan(step, init, (to_chunks(q), to_chunks(k), to_chunks(v), to_chunks(logi), to_chunks(logf)))
    return jnp.moveaxis(hs, (0, 2), (1, 3)).reshape(B, S, H, DV)


def mla_attention(q_nope, q_rope, k_nope, k_rope, v):
    B, S, H, _ = q_nope.shape
    scale = (B_NOPE + B_ROPE) ** -0.5
    kpos = jnp.arange(S)

    def block(i):
        t0 = i * Q_BLOCK
        qn = lax.dynamic_slice_in_dim(q_nope, t0, Q_BLOCK, axis=1)
        qr = lax.dynamic_slice_in_dim(q_rope, t0, Q_BLOCK, axis=1)
        s = (jnp.einsum('bqhd,bkhd->bhqk', qn, k_nope) + jnp.einsum('bqhd,bkd->bhqk', qr, k_rope)) * scale
        mask = (t0 + jnp.arange(Q_BLOCK))[:, None] >= kpos[None, :]
        p = masked_softmax(s, mask)
        return jnp.einsum('bhqk,bkhd->bqhd', p.astype(v.dtype), v)

    out = lax.map(block, jnp.arange(S // Q_BLOCK))
    return jnp.moveaxis(out, 0, 1).reshape(B, S, H, v.shape[-1])


def mixer_ab(x, w_in, b_igate, b_fgate, mlstm_norm, q_norm, kv_norm, w_uq, w_ukv, w_o):
    B, S, _ = x.shape
    f32 = jnp.float32
    proj = x @ w_in
    q_a, k_a, v_a, ig, fg, og, cq, ckv, kr = jnp.split(proj, AB_SPLITS, axis=-1)
    qa = q_a.reshape(B, S, A_HEADS, A_DQK).astype(f32)
    ka = k_a.reshape(B, S, A_HEADS, A_DQK).astype(f32) * (A_DQK ** -0.5)
    va = v_a.reshape(B, S, A_HEADS, A_DV).astype(f32)
    logi = soft_cap(ig.astype(f32) + b_igate)
    logf = jax.nn.log_sigmoid(soft_cap(fg.astype(f32) + b_fgate))
    h_a = mlstm_chunkwise(qa, ka, va, logi, logf)
    h_a = rms_norm(h_a, mlstm_norm.reshape(A_HEADS, A_DV)).reshape(B, S, A_V)
    h_a = h_a * jax.nn.sigmoid(og.astype(f32))
    pos = jnp.arange(S)
    qb = (rms_norm(cq, q_norm) @ w_uq).reshape(B, S, B_HEADS, B_NOPE + B_ROPE)
    q_nope, q_rope = qb[..., :B_NOPE], rope(qb[..., B_NOPE:], pos)
    kvb = (rms_norm(ckv, kv_norm) @ w_ukv).reshape(B, S, B_HEADS, B_NOPE + B_DV)
    k_nope, v_b = kvb[..., :B_NOPE], kvb[..., B_NOPE:]
    k_rope = rope(kr.reshape(B, S, 1, B_ROPE), pos)[:, :, 0]
    h_b = mla_attention(q_nope, q_rope, k_nope, k_rope, v_b).reshape(B, S, B_HEADS * B_DV)
    h = jnp.concatenate([h_a.astype(x.dtype), h_b.astype(x.dtype)], -1)
    return h @ w_o


def mixer_c(x, w_in, b_gate, pe_k, pe_v, cmp_w1_k, cmp_w2_k, cmp_w1_v, cmp_w2_v, w_o):
    B, S, _ = x.shape
    G, HG, HD = C_KV_GROUPS, C_HEADS // C_KV_GROUPS, C_HD
    f32 = jnp.float32
    scale = HD ** -0.5
    proj = x @ w_in
    q, kc, vc, ks, vs, kw, vw, gates = jnp.split(proj, C_SPLITS, axis=-1)
    q = q.reshape(B, S, G, HG, HD)
    kc, vc, ks, vs, kw, vw = [t.reshape(B, S, G, HD) for t in (kc, vc, ks, vs, kw, vw)]
    gates = jax.nn.sigmoid(gates.astype(f32) + b_gate).reshape(B, S, 3, G, HG)

    n_cmp = (S - C_CMP_LEN) // C_CMP_STRIDE + 1
    cmp_start = jnp.arange(n_cmp) * C_CMP_STRIDE
    blk_idx = cmp_start[:, None] + jnp.arange(C_CMP_LEN)[None, :]
    cmp_end = cmp_start + C_CMP_LEN - 1

    def compress(t, pe, w1, w2):
        blocks = t[:, blk_idx] + pe[:, None, :]
        blocks = jnp.moveaxis(blocks, 3, 2).reshape(B, n_cmp, G, C_CMP_LEN * HD)
        return jax.nn.gelu(blocks @ w1) @ w2

    k_cmp = compress(kc, pe_k, cmp_w1_k, cmp_w2_k)
    v_cmp = compress(vc, pe_v, cmp_w1_v, cmp_w2_v)

    n_sel = S // C_SEL_LEN
    n_top = min(C_N_SEL, n_sel)
    sel_start = jnp.arange(n_sel) * C_SEL_LEN
    overlap = ((cmp_start[:, None] < sel_start[None, :] + C_SEL_LEN)
               & (cmp_start[:, None] + C_CMP_LEN > sel_start[None, :])).astype(f32)
    ks_blocks = jnp.moveaxis(ks.reshape(B, n_sel, C_SEL_LEN, G, HD), 3, 1)
    vs_blocks = jnp.moveaxis(vs.reshape(B, n_sel, C_SEL_LEN, G, HD), 3, 1)
    bi = jnp.arange(B)[:, None, None, None]
    gi = jnp.arange(G)[None, :, None, None]
    sblk = jnp.arange(n_sel)

    pad = ((0, 0), (C_WINDOW, 0), (0, 0), (0, 0))
    kw_pad, vw_pad = jnp.pad(kw, pad), jnp.pad(vw, pad)
    band = C_WINDOW + C_Q_BLOCK

    def block(i):
        t0 = i * C_Q_BLOCK
        tq = t0 + jnp.arange(C_Q_BLOCK)
        qb = lax.dynamic_slice_in_dim(q, t0, C_Q_BLOCK, 1).astype(f32)
        gb = lax.dynamic_slice_in_dim(gates, t0, C_Q_BLOCK, 1)
        s_c = jnp.einsum('bqghd,bngd->bghqn', qb, k_cmp) * scale
        p_c = masked_softmax(s_c, cmp_end[None, :] <= tq[:, None])
        o_c = jnp.einsum('bghqn,bngd->bqghd', p_c, v_cmp)
        imp = jnp.einsum('bghqn,ns->bgqs', p_c, overlap)
        cur = tq // C_SEL_LEN
        forced = (sblk[None, :] == 0) | (sblk[None, :] == cur[:, None]) | (sblk[None, :] == cur[:, None] - 1)
        imp = jnp.where(forced, FORCE_SCORE, imp)
        imp = jnp.where(sblk[None, :] <= cur[:, None], imp, NEG_INF)
        _, sel = lax.top_k(imp, n_top)
        k_sel = ks_blocks[bi, gi, sel]
        v_sel = vs_blocks[bi, gi, sel]
        s_s = jnp.einsum('bqghd,bgqnld->bghqnl', qb, k_sel) * scale
        kpos_s = sel[..., None] * C_SEL_LEN + jnp.arange(C_SEL_LEN)
        m_s = (kpos_s <= tq[None, None, :, None, None])[:, :, None]
        p_s = masked_softmax(s_s, m_s, axis=(-2, -1))
        o_s = jnp.einsum('bghqnl,bgqnld->bqghd', p_s, v_sel)
        kwb = lax.dynamic_slice_in_dim(kw_pad, t0, band, 1)
        vwb = lax.dynamic_slice_in_dim(vw_pad, t0, band, 1)
        kpos_w = t0 - C_WINDOW + jnp.arange(band)
        m_w = ((kpos_w[None, :] <= tq[:, None]) & (kpos_w[None, :] > tq[:, None] - C_WINDOW)
               & (kpos_w[None, :] >= 0))
        s_w = jnp.einsum('bqghd,bkgd->bghqk', qb, kwb) * scale
        p_w = masked_softmax(s_w, m_w)
        o_w = jnp.einsum('bghqk,bkgd->bqghd', p_w, vwb)
        return (gb[:, :, 0, :, :, None] * o_c + gb[:, :, 1, :, :, None] * o_s
                + gb[:, :, 2, :, :, None] * o_w)

    out = lax.map(block, jnp.arange(S // C_Q_BLOCK))
    out = jnp.moveaxis(out, 0, 1).reshape(B, S, C_HEADS * HD).astype(x.dtype)
    return out @ w_o


def swiglu(x, w_gate, w_up, w_down):
    return (jax.nn.silu(x @ w_gate) * (x @ w_up)) @ w_down


def setup_inputs(seed: int = 0) -> dict:
    key = jax.random.key(seed)
    ks = iter(jax.random.split(key, 40))
    NE, NO = (DEPTH + 1) // 2, DEPTH // 2

    def nrm(shape, scale):
        return jax.random.normal(next(ks), shape, jnp.float32) * scale

    def gain(shape):
        return 1.0 + nrm(shape, 0.02)

    return {
        "x": nrm((BATCH, SEQ, D_MODEL), 1.0),
        "ab_w_in": nrm((NE, D_MODEL, AB_IN), D_MODEL ** -0.5),
        "ab_b_igate": nrm((NE, A_HEADS), 0.1),
        "ab_b_fgate": 3.0 + 3.0 * jax.random.uniform(next(ks), (NE, A_HEADS), jnp.float32),
        "ab_mlstm_norm": gain((NE, A_V)),
        "ab_q_norm": gain((NE, B_Q_LORA)),
        "ab_kv_norm": gain((NE, B_KV_LORA)),
        "ab_w_uq": nrm((NE, B_Q_LORA, B_HEADS * (B_NOPE + B_ROPE)), B_Q_LORA ** -0.5),
        "ab_w_ukv": nrm((NE, B_KV_LORA, B_HEADS * (B_NOPE + B_DV)), B_KV_LORA ** -0.5),
        "ab_w_o": nrm((NE, MIX_WIDTH, D_MODEL), DN_BETA * MIX_WIDTH ** -0.5),
        "c_w_in": nrm((NO, D_MODEL, C_IN), D_MODEL ** -0.5),
        "c_b_gate": nrm((NO, 3 * C_HEADS), 0.1),
        "c_pe_k": nrm((NO, C_CMP_LEN, C_HD), 0.1),
        "c_pe_v": nrm((NO, C_CMP_LEN, C_HD), 0.1),
        "c_cmp_w1_k": nrm((NO, C_CMP_LEN * C_HD, C_CMP_HIDDEN), (C_CMP_LEN * C_HD) ** -0.5),
        "c_cmp_w2_k": nrm((NO, C_CMP_HIDDEN, C_HD), C_CMP_HIDDEN ** -0.5),
        "c_cmp_w1_v": nrm((NO, C_CMP_LEN * C_HD, C_CMP_HIDDEN), (C_CMP_LEN * C_HD) ** -0.5),
        "c_cmp_w2_v": nrm((NO, C_CMP_HIDDEN, C_HD), C_CMP_HIDDEN ** -0.5),
        "c_w_o": nrm((NO, MIX_WIDTH, D_MODEL), DN_BETA * MIX_WIDTH ** -0.5),
        "ffn_w_gate": nrm((DEPTH, D_MODEL, FFN_HIDDEN), D_MODEL ** -0.5),
        "ffn_w_up": nrm((DEPTH, D_MODEL, FFN_HIDDEN), D_MODEL ** -0.5),
        "ffn_w_down": nrm((DEPTH, FFN_HIDDEN, D_MODEL), DN_BETA * FFN_HIDDEN ** -0.5),
        "ln_mix_g": gain((DEPTH, D_MODEL)),
        "ln_mix_b": nrm((DEPTH, D_MODEL), 0.02),
        "ln_ffn_g": gain((DEPTH, D_MODEL)),
        "ln_ffn_b": nrm((DEPTH, D_MODEL), 0.02),
    }


def reference(x, ab_w_in, ab_b_igate, ab_b_fgate, ab_mlstm_norm, ab_q_norm, ab_kv_norm, ab_w_uq, ab_w_ukv,
              ab_w_o, c_w_in, c_b_gate, c_pe_k, c_pe_v, c_cmp_w1_k, c_cmp_w2_k, c_cmp_w1_v, c_cmp_w2_v, c_w_o,
              ffn_w_gate, ffn_w_up, ffn_w_down, ln_mix_g, ln_mix_b, ln_ffn_g, ln_ffn_b):
    for layer in range(DEPTH):
        j = layer // 2
        if layer % 2 == 0:
            y = mixer_ab(x, ab_w_in[j], ab_b_igate[j], ab_b_fgate[j], ab_mlstm_norm[j], ab_q_norm[j],
                         ab_kv_norm[j], ab_w_uq[j], ab_w_ukv[j], ab_w_o[j])
        else:
            y = mixer_c(x, c_w_in[j], c_b_gate[j], c_pe_k[j], c_pe_v[j], c_cmp_w1_k[j], c_cmp_w2_k[j],
                        c_cmp_w1_v[j], c_cmp_w2_v[j], c_w_o[j])
        x = layer_norm(DN_ALPHA * x + y, ln_mix_g[layer], ln_mix_b[layer])
        x = layer_norm(DN_ALPHA * x + swiglu(x, ffn_w_gate[layer], ffn_w_up[layer], ffn_w_down[layer]),
                       ln_ffn_g[layer], ln_ffn_b[layer])
    return x
```

```python
import functools
import math

import numpy as np
import jax
import jax.numpy as jnp
from jax import lax
from jax.experimental import pallas as pl
from jax.experimental.pallas import tpu as pltpu

F32 = jnp.float32
BF16 = jnp.bfloat16

D_MODEL = 4096
DEPTH = 2
DN_ALPHA = (2 * DEPTH) ** 0.25
LN_EPS = 1e-5
RMS_EPS = 1e-6
NEG_INF = -1e30
FORCE_SCORE = 1e9

A_HEADS, A_DQK, A_DV = 4, 256, 512
A_QK, A_V = A_HEADS * A_DQK, A_HEADS * A_DV
A_GATE_CAP = 15.0
B_HEADS, B_Q_LORA, B_KV_LORA, B_NOPE, B_ROPE, B_DV = 16, 1024, 512, 128, 64, 128
ROPE_THETA = 10000.0
C_HEADS, C_KV_GROUPS, C_HD = 32, 4, 128
C_HG = C_HEADS // C_KV_GROUPS
C_KV = C_KV_GROUPS * C_HD
C_CMP_LEN, C_CMP_STRIDE, C_SEL_LEN, C_N_SEL, C_WINDOW, C_CMP_HIDDEN = 32, 16, 64, 16, 512, 256
C_Q_BLOCK = 64
FFN_HIDDEN = 11008
FFN_PAD = 11264

AB_SPLITS = [int(v) for v in np.cumsum([A_QK, A_QK, A_V, A_HEADS, A_HEADS, A_V, B_Q_LORA, B_KV_LORA])]
C_SPLITS = [int(v) for v in np.cumsum([C_HEADS * C_HD] + [C_KV] * 6)]

LANES = 128
VMEM_LIMIT = 56 * 1024 * 1024

P0_Q, P0_K, P0_V, P0_OG, P0_CQ, P0_CKV, P0_N = 0, 1024, 2048, 4096, 6144, 7168, 7680
MISC_IG, MISC_FG = 64, 68
P1_Q, P1_KC, P1_VC, P1_KS, P1_VS, P1_KW, P1_VW, P1_G, P1_N = 0, 4096, 4608, 5120, 5632, 6144, 6656, 7168, 7680

MLSTM_CHUNK = 256
SEL_TILE = 256
N_SEL_BLOCKS_MAX = 64


def _params(sem):
    return pltpu.CompilerParams(dimension_semantics=sem, vmem_limit_bytes=VMEM_LIMIT)


def _mm_kernel(a_ref, b_ref, o_ref):
    o_ref[...] = jnp.dot(a_ref[...], b_ref[...], preferred_element_type=F32).astype(o_ref.dtype)


def _matmul(a, b, out_dtype, tm, tn, name):
    m, k = a.shape
    n = b.shape[1]
    assert m % tm == 0 and n % tn == 0
    return pl.pallas_call(
        _mm_kernel,
        grid=(m // tm, n // tn),
        in_specs=[pl.BlockSpec((tm, k), lambda i, j: (i, 0)),
                  pl.BlockSpec((k, tn), lambda i, j: (0, j))],
        out_specs=pl.BlockSpec((tm, tn), lambda i, j: (i, j)),
        out_shape=jax.ShapeDtypeStruct((m, n), out_dtype),
        compiler_params=_params(("parallel", "arbitrary")),
        name=name,
    )(a, b)


def _mm_acc_kernel(a_ref, b_ref, o_ref):
    k = pl.program_id(2)

    @pl.when(k == 0)
    def _():
        o_ref[...] = jnp.dot(a_ref[...], b_ref[...], preferred_element_type=F32)

    @pl.when(k > 0)
    def _():
        o_ref[...] += jnp.dot(a_ref[...], b_ref[...], preferred_element_type=F32)


def _matmul_kgrid(a, b, tm, tn, tk, name):
    m, k = a.shape
    n = b.shape[1]
    assert m % tm == 0 and n % tn == 0 and k % tk == 0
    return pl.pallas_call(
        _mm_acc_kernel,
        grid=(m // tm, n // tn, k // tk),
        in_specs=[pl.BlockSpec((tm, tk), lambda i, j, kk: (i, kk)),
                  pl.BlockSpec((tk, tn), lambda i, j, kk: (kk, j))],
        out_specs=pl.BlockSpec((tm, tn), lambda i, j, kk: (i, j)),
        out_shape=jax.ShapeDtypeStruct((m, n), F32),
        compiler_params=_params(("parallel", "arbitrary", "arbitrary")),
        name=name,
    )(a, b)


def _misc_kernel(x_ref, wh_ref, wl_ref, o_ref):
    x = x_ref[...]
    xh = x.astype(BF16)
    xl = (x - xh.astype(F32)).astype(BF16)
    wh = wh_ref[...]
    acc = jnp.dot(xh, wh, preferred_element_type=F32)
    acc += jnp.dot(xl, wh, preferred_element_type=F32)
    acc += jnp.dot(xh, wl_ref[...], preferred_element_type=F32)
    o_ref[...] = acc


def _misc_proj(x, w_hi, w_lo, tm):
    m, k = x.shape
    n = w_hi.shape[1]
    return pl.pallas_call(
        _misc_kernel,
        grid=(m // tm,),
        in_specs=[pl.BlockSpec((tm, k), lambda i: (i, 0)),
                  pl.BlockSpec((k, n), lambda i: (0, 0)),
                  pl.BlockSpec((k, n), lambda i: (0, 0))],
        out_specs=pl.BlockSpec((tm, n), lambda i: (i, 0)),
        out_shape=jax.ShapeDtypeStruct((m, n), F32),
        compiler_params=_params(("parallel",)),
        name="misc_proj",
    )(x, w_hi, w_lo)


def _swiglu_up_kernel(x_ref, wg_ref, wu_ref, o_ref):
    x = x_ref[...]
    g = jnp.dot(x, wg_ref[...], preferred_element_type=F32)
    u = jnp.dot(x, wu_ref[...], preferred_element_type=F32)
    o_ref[...] = (g * jax.nn.sigmoid(g) * u).astype(o_ref.dtype)


def _swiglu_up(x, wg, wu, tm, tn):
    m, k = x.shape
    n = wg.shape[1]
    return pl.pallas_call(
        _swiglu_up_kernel,
        grid=(m // tm, n // tn),
        in_specs=[pl.BlockSpec((tm, k), lambda i, j: (i, 0)),
                  pl.BlockSpec((k, tn), lambda i, j: (0, j)),
                  pl.BlockSpec((k, tn), lambda i, j: (0, j))],
        out_specs=pl.BlockSpec((tm, tn), lambda i, j: (i, j)),
        out_shape=jax.ShapeDtypeStruct((m, n), BF16),
        compiler_params=_params(("parallel", "arbitrary")),
        name="swiglu_up",
    )(x, wg, wu)


def _add_ln_kernel(x_ref, y_ref, g_ref, b_ref, o_ref, ob_ref):
    z = DN_ALPHA * x_ref[...] + y_ref[...]
    mu = jnp.mean(z, axis=-1, keepdims=True)
    zc = z - mu
    var = jnp.mean(zc * zc, axis=-1, keepdims=True)
    out = zc * lax.rsqrt(var + LN_EPS) * g_ref[...] + b_ref[...]
    o_ref[...] = out
    ob_ref[...] = out.astype(BF16)


def _add_ln(x, y, g, b, tm=256):
    m, d = x.shape
    row = pl.BlockSpec((tm, d), lambda i: (i, 0))
    vec = pl.BlockSpec((1, d), lambda i: (0, 0))
    return pl.pallas_call(
        _add_ln_kernel,
        grid=(m // tm,),
        in_specs=[row, row, vec, vec],
        out_specs=[row, row],
        out_shape=[jax.ShapeDtypeStruct((m, d), F32), jax.ShapeDtypeStruct((m, d), BF16)],
        compiler_params=_params(("parallel",)),
        name="add_ln",
    )(x, y, g.reshape(1, d), b.reshape(1, d))


def _soft_cap(z):
    return A_GATE_CAP * jnp.tanh(z / A_GATE_CAP)


def _log_sigmoid(z):
    return jnp.minimum(z, 0.0) - jnp.log1p(jnp.exp(-jnp.abs(z)))


def _mlstm_kernel(q_ref, k_ref, v_ref, og_ref, gr_ref, gc_ref, bias_ref, norm_ref, o_ref,
                  c_ref, n_ref, m_ref):
    L = q_ref.shape[0]
    c = pl.program_id(2)

    @pl.when(c == 0)
    def _():
        c_ref[...] = jnp.zeros_like(c_ref)
        n_ref[...] = jnp.zeros_like(n_ref)
        m_ref[...] = jnp.zeros_like(m_ref)

    bias = bias_ref[...]
    gr = gr_ref[...]
    gc = gc_ref[...]
    li_r = _soft_cap(gr[0:1, :] + bias[:, 0:1])
    lf_r = _log_sigmoid(_soft_cap(gr[1:2, :] + bias[:, 1:2]))
    li_c = _soft_cap(gc[:, 0:1] + bias[:, 0:1])
    lf_c = _log_sigmoid(_soft_cap(gc[:, 1:2] + bias[:, 1:2]))

    t_idx = lax.broadcasted_iota(jnp.int32, (L, L), 0)
    s_idx = lax.broadcasted_iota(jnp.int32, (L, L), 1)
    causal = s_idx <= t_idx
    b_c = jnp.sum(jnp.where(causal, lf_r, 0.0), axis=1, keepdims=True)
    b_r = jnp.sum(jnp.where(t_idx <= s_idx, lf_c, 0.0), axis=0, keepdims=True)
    g_tot = jnp.sum(lf_r, axis=1, keepdims=True)

    m_prev = m_ref[...]
    dmat = jnp.where(causal, b_c - b_r + li_r, NEG_INF)
    m_inter = b_c + m_prev
    m_t = jnp.maximum(m_inter, jnp.max(dmat, axis=1, keepdims=True))
    w_inter = jnp.exp(m_inter - m_t)
    pmat = jnp.exp(dmat - m_t)

    q = q_ref[...]
    k = k_ref[...] * (A_DQK ** -0.5)
    v = v_ref[...]
    qk = lax.dot_general(q, k, (((1,), (1,)), ((), ())), preferred_element_type=F32)
    s = qk * pmat
    num = w_inter * jnp.dot(q, c_ref[...].astype(BF16), preferred_element_type=F32)
    num += jnp.dot(s.astype(BF16), v, preferred_element_type=F32)
    den = w_inter * jnp.sum(q.astype(F32) * n_ref[...], axis=1, keepdims=True)
    den += jnp.sum(s, axis=1, keepdims=True)
    h = num / jnp.maximum(jnp.abs(den), jnp.exp(-m_t))

    hn = h * lax.rsqrt(jnp.mean(h * h, axis=-1, keepdims=True) + RMS_EPS) * norm_ref[...]
    o_ref[...] = (hn * jax.nn.sigmoid(og_ref[...].astype(F32))).astype(o_ref.dtype)

    ws_c = g_tot - b_c + li_c
    ws_r = g_tot - b_r + li_r
    m_new = jnp.maximum(g_tot + m_prev, jnp.max(ws_r, axis=1, keepdims=True))
    decay = jnp.exp(g_tot + m_prev - m_new)
    kw = k.astype(F32) * jnp.exp(ws_c - m_new)
    c_ref[...] = decay * c_ref[...] + lax.dot_general(
        kw.astype(BF16), v, (((0,), (0,)), ((), ())), preferred_element_type=F32)
    n_ref[...] = decay * n_ref[...] + jnp.sum(kw, axis=0, keepdims=True)
    m_ref[...] = m_new


def _mlstm(p0, gates_r, gates_c, bias, norm, bsz, seq):
    L = MLSTM_CHUNK
    nc = seq // L
    qb, kb, vb, ogb = P0_Q // A_DQK, P0_K // A_DQK, P0_V // A_DV, P0_OG // A_DV
    return pl.pallas_call(
        _mlstm_kernel,
        grid=(bsz, A_HEADS, nc),
        in_specs=[
            pl.BlockSpec((None, L, A_DQK), lambda b, h, c: (b, c, qb + h)),
            pl.BlockSpec((None, L, A_DQK), lambda b, h, c: (b, c, kb + h)),
            pl.BlockSpec((None, L, A_DV), lambda b, h, c: (b, c, vb + h)),
            pl.BlockSpec((None, L, A_DV), lambda b, h, c: (b, c, ogb + h)),
            pl.BlockSpec((None, None, None, 2, L), lambda b, h, c: (b, h, c, 0, 0)),
            pl.BlockSpec((None, None, L, 2), lambda b, h, c: (b, h, c, 0)),
            pl.BlockSpec((None, 1, 2), lambda b, h, c: (h, 0, 0)),
            pl.BlockSpec((None, 1, A_DV), lambda b, h, c: (h, 0, 0)),
        ],
        out_specs=pl.BlockSpec((None, L, A_DV), lambda b, h, c: (b, c, h)),
        out_shape=jax.ShapeDtypeStruct((bsz, seq, A_V), BF16),
        scratch_shapes=[pltpu.VMEM((A_DQK, A_DV), F32), pltpu.VMEM((1, A_DQK), F32), pltpu.VMEM((1, 1), F32)],
        compiler_params=_params(("parallel", "parallel", "arbitrary")),
        name="mlstm",
    )(p0, p0, p0, p0, gates_r, gates_c, bias, norm)


def _rms(x, g):
    return x * lax.rsqrt(jnp.mean(x * x, axis=-1, keepdims=True) + RMS_EPS) * g


def _rope_tile(x, cos, sina, sinb):
    return x * cos + pltpu.roll(x, 96, 1) * sina + pltpu.roll(x, 32, 1) * sinb


def _mla_prep_kernel(cq_ref, ckv_ref, kr_ref, qg_ref, kvg_ref, wuq_ref, wuk_ref, wuv_ref,
                     cos_ref, sina_ref, sinb_ref, q_ref, k_ref, v_ref):
    scale = (B_NOPE + B_ROPE) ** -0.5
    cos, sina, sinb = cos_ref[...], sina_ref[...], sinb_ref[...]
    cqn = _rms(cq_ref[...].astype(F32), qg_ref[...]).astype(BF16)
    qb = jnp.dot(cqn, wuq_ref[...], preferred_element_type=F32)
    ckvn = _rms(ckv_ref[...].astype(F32), kvg_ref[...]).astype(BF16)
    kup = jnp.dot(ckvn, wuk_ref[...], preferred_element_type=F32)
    v_ref[...] = jnp.dot(ckvn, wuv_ref[...], preferred_element_type=F32).astype(BF16)
    krope = _rope_tile(kr_ref[...], cos, sina, sinb).astype(BF16)
    for h in range(B_HEADS):
        c0 = 2 * LANES * h
        q_ref[:, c0:c0 + LANES] = (qb[:, c0:c0 + LANES] * scale).astype(BF16)
        q_ref[:, c0 + LANES:c0 + 2 * LANES] = (
            _rope_tile(qb[:, c0 + LANES:c0 + 2 * LANES], cos, sina, sinb) * scale).astype(BF16)
        k_ref[:, c0:c0 + LANES] = kup[:, LANES * h:LANES * (h + 1)].astype(BF16)
        k_ref[:, c0 + LANES:c0 + 2 * LANES] = krope


def _mla_prep(p0, misc, qg, kvg, wuq, wuk, wuv, cos, sina, sinb, seq, tm):
    m = p0.shape[0]
    nrope = seq // tm
    hq = B_HEADS * 2 * LANES
    hv = B_HEADS * B_DV
    full = lambda shape: pl.BlockSpec(shape, lambda i: (0, 0))
    tab = pl.BlockSpec((tm, LANES), lambda i: (i % nrope, 0))
    return pl.pallas_call(
        _mla_prep_kernel,
        grid=(m // tm,),
        in_specs=[
            pl.BlockSpec((tm, B_Q_LORA), lambda i: (i, P0_CQ // B_Q_LORA)),
            pl.BlockSpec((tm, B_KV_LORA), lambda i: (i, P0_CKV // B_KV_LORA)),
            pl.BlockSpec((tm, LANES), lambda i: (i, 0)),
            full((1, B_Q_LORA)), full((1, B_KV_LORA)),
            full((B_Q_LORA, hq)), full((B_KV_LORA, hv)), full((B_KV_LORA, hv)),
            tab, tab, tab,
        ],
        out_specs=[pl.BlockSpec((tm, hq), lambda i: (i, 0)),
                   pl.BlockSpec((tm, hq), lambda i: (i, 0)),
                   pl.BlockSpec((tm, hv), lambda i: (i, 0))],
        out_shape=[jax.ShapeDtypeStruct((m, hq), BF16), jax.ShapeDtypeStruct((m, hq), BF16),
                   jax.ShapeDtypeStruct((m, hv), BF16)],
        compiler_params=_params(("parallel",)),
        name="mla_prep",
    )(p0, p0, misc, qg, kvg, wuq, wuk, wuv, cos, sina, sinb)


def _softmax_step(s, valid, v, m_ref, l_ref, acc_ref):
    m_prev = m_ref[...]
    if valid is None:
        m_new = jnp.maximum(m_prev, jnp.max(s, axis=1, keepdims=True))
        p = jnp.exp(s - m_new)
    else:
        m_new = jnp.maximum(m_prev, jnp.max(jnp.where(valid, s, NEG_INF), axis=1, keepdims=True))
        p = jnp.where(valid, jnp.exp(s - m_new), 0.0)
    alpha = jnp.exp(m_prev - m_new)
    l_ref[...] = alpha * l_ref[...] + jnp.sum(p, axis=1, keepdims=True)
    acc_ref[...] = alpha * acc_ref[...] + jnp.dot(p.astype(BF16), v, preferred_element_type=F32)
    m_ref[...] = m_new


def _softmax_init(m_ref, l_ref, acc_ref):
    m_ref[...] = jnp.full_like(m_ref, NEG_INF)
    l_ref[...] = jnp.zeros_like(l_ref)
    acc_ref[...] = jnp.zeros_like(acc_ref)


def _mla_attn_kernel(q_ref, k_ref, v_ref, o_ref, m_ref, l_ref, acc_ref):
    tq, tk = q_ref.shape[0], k_ref.shape[0]
    i, j = pl.program_id(2), pl.program_id(3)

    @pl.when(j == 0)
    def _():
        _softmax_init(m_ref, l_ref, acc_ref)

    def scores():
        return lax.dot_general(q_ref[...], k_ref[...], (((1,), (1,)), ((), ())), preferred_element_type=F32)

    @pl.when(j < i)
    def _():
        _softmax_step(scores(), None, v_ref[...], m_ref, l_ref, acc_ref)

    @pl.when(j == i)
    def _():
        rows = lax.broadcasted_iota(jnp.int32, (tq, tk), 0)
        cols = lax.broadcasted_iota(jnp.int32, (tq, tk), 1)
        _softmax_step(scores(), cols <= rows, v_ref[...], m_ref, l_ref, acc_ref)
        o_ref[...] = (acc_ref[...] / l_ref[...]).astype(o_ref.dtype)


def _mla_attn(qc, kc, vv, bsz, seq, t):
    n = seq // t
    kv_idx = lambda b, h, i, j: (b, jnp.minimum(j, i), h)
    return pl.pallas_call(
        _mla_attn_kernel,
        grid=(bsz, B_HEADS, n, n),
        in_specs=[pl.BlockSpec((None, t, 2 * LANES), lambda b, h, i, j: (b, i, h)),
                  pl.BlockSpec((None, t, 2 * LANES), kv_idx),
                  pl.BlockSpec((None, t, B_DV), kv_idx)],
        out_specs=pl.BlockSpec((None, t, B_DV), lambda b, h, i, j: (b, i, h)),
        out_shape=jax.ShapeDtypeStruct((bsz, seq, B_HEADS * B_DV), BF16),
        scratch_shapes=[pltpu.VMEM((t, 1), F32), pltpu.VMEM((t, 1), F32), pltpu.VMEM((t, B_DV), F32)],
        compiler_params=_params(("parallel", "parallel", "parallel", "arbitrary")),
        name="mla_attn",
    )(qc, kc, vv)


def _gelu_tanh(x):
    return 0.5 * x * (1.0 + jnp.tanh(math.sqrt(2.0 / math.pi) * (x + 0.044715 * (x * x * x))))


def _compress_kernel(seg_ref, pe_ref, w1_ref, w2_ref, o_ref):
    half = seg_ref.shape[1]
    nseg = seg_ref.shape[0]
    seg = seg_ref[...].astype(F32)
    pe = pe_ref[...]
    top = (seg + pe[:, :half]).astype(BF16)
    bot = (seg + pe[:, half:]).astype(BF16)
    u = jnp.dot(top, w1_ref[:half, :], preferred_element_type=F32)
    w = jnp.dot(bot, w1_ref[half:, :], preferred_element_type=F32)
    hid = u + pltpu.roll(w, nseg - 1, 0)
    o_ref[...] = jnp.dot(_gelu_tanh(hid).astype(BF16), w2_ref[...], preferred_element_type=F32).astype(BF16)


def _compress(segs, pe, w1, w2):
    _, bsz, g, nseg, width = segs.shape
    return pl.pallas_call(
        _compress_kernel,
        grid=(2, bsz, g),
        in_specs=[pl.BlockSpec((None, None, None, nseg, width), lambda t, b, gg: (t, b, gg, 0, 0)),
                  pl.BlockSpec((None, 1, 2 * width), lambda t, b, gg: (t, 0, 0)),
                  pl.BlockSpec((None, 2 * width, C_CMP_HIDDEN), lambda t, b, gg: (t, 0, 0)),
                  pl.BlockSpec((None, C_CMP_HIDDEN, C_HD), lambda t, b, gg: (t, 0, 0))],
        out_specs=pl.BlockSpec((None, None, None, nseg, C_HD), lambda t, b, gg: (t, b, gg, 0, 0)),
        out_shape=jax.ShapeDtypeStruct((2, bsz, g, nseg, C_HD), BF16),
        compiler_params=_params(("parallel", "parallel", "parallel")),
        name="nsa_compress",
    )(segs, pe, w1, w2)


def _split3_dot(x, w):
    x1 = x.astype(BF16)
    r1 = x - x1.astype(F32)
    x2 = r1.astype(BF16)
    x3 = (r1 - x2.astype(F32)).astype(BF16)
    out = jnp.dot(x1, w, preferred_element_type=F32)
    out += jnp.dot(x2, w, preferred_element_type=F32)
    out += jnp.dot(x3, w, preferred_element_type=F32)
    return out


def _nsa_kernel(q_ref, gate_ref, bg_ref, kcmp_ref, vcmp_ref, ks_ref, vs_ref, kw_ref, vw_ref,
                ovl_ref, exp_ref, o_ref, mexp_ref, m_ref, l_ref, acc_ref):
    QB, HG, HD = C_Q_BLOCK, C_HG, C_HD
    R = QB * HG
    i = pl.program_id(2)
    t0 = i * QB
    scale = HD ** -0.5
    nt = (((1,), (1,)), ((), ()))

    q_all = q_ref[...]
    q = jnp.concatenate([q_all[:, h * HD:(h + 1) * HD] for h in range(HG)], axis=0)
    q = (q.astype(F32) * scale).astype(BF16)
    tq = t0 + lax.broadcasted_iota(jnp.int32, (HG, QB, 1), 1).reshape(R, 1)

    ncmp = kcmp_ref.shape[0]
    s_c = lax.dot_general(q, kcmp_ref[...], nt, preferred_element_type=F32)
    cmp_end = lax.broadcasted_iota(jnp.int32, (R, ncmp), 1) * C_CMP_STRIDE + (C_CMP_LEN - 1)
    vis = cmp_end <= tq
    s_m = jnp.where(vis, s_c, NEG_INF)
    e = jnp.where(vis, jnp.exp(s_m - jnp.max(s_m, axis=1, keepdims=True)), 0.0)
    l_c = jnp.sum(e, axis=1, keepdims=True)
    p_c = e / jnp.where(l_c > 0.0, l_c, 1.0)
    o_c = jnp.dot(p_c.astype(BF16), vcmp_ref[...], preferred_element_type=F32)

    p_sum = p_c[0:QB]
    for h in range(1, HG):
        p_sum = p_sum + p_c[h * QB:(h + 1) * QB]
    imp = _split3_dot(p_sum, ovl_ref[...])
    nsel = imp.shape[1]
    sblk = lax.broadcasted_iota(jnp.int32, (QB, nsel), 1)
    forced = (sblk == 0) | (sblk == i) | (sblk == i - 1)
    imp = jnp.where(forced, FORCE_SCORE, imp)
    imp = jnp.where(sblk <= i, imp, NEG_INF)
    rank = jnp.zeros((QB, nsel), F32)
    for sp in range(nsel):
        col = imp[:, sp:sp + 1]
        beats = (col > imp) | ((col == imp) & (sblk > sp))
        rank = rank + jnp.where(beats, 1.0, 0.0)
    sel = jnp.where(rank < float(C_N_SEL), 1.0, 0.0).astype(BF16)
    n_tiles_max = mexp_ref.shape[0]
    mfull = jnp.dot(sel, exp_ref[...], preferred_element_type=F32)
    for j in range(n_tiles_max):
        mexp_ref[j] = mfull[:, j * SEL_TILE:(j + 1) * SEL_TILE]

    _softmax_init(m_ref, l_ref, acc_ref)
    lane = lax.broadcasted_iota(jnp.int32, (R, SEL_TILE), 1)

    def sel_body(j, carry):
        k0 = pl.multiple_of(j * SEL_TILE, SEL_TILE)
        s = lax.dot_general(q, ks_ref[pl.ds(k0, SEL_TILE), :], nt, preferred_element_type=F32)
        msk = jnp.broadcast_to(mexp_ref[j][None], (HG, QB, SEL_TILE)).reshape(R, SEL_TILE)
        valid = (msk > 0.5) & (k0 + lane <= tq)
        _softmax_step(s, valid, vs_ref[pl.ds(k0, SEL_TILE), :], m_ref, l_ref, acc_ref)
        return carry

    lax.fori_loop(0, (t0 + QB + SEL_TILE - 1) // SEL_TILE, sel_body, 0)
    o_s = acc_ref[...] / l_ref[...]

    _softmax_init(m_ref, l_ref, acc_ref)
    n_win = (C_WINDOW + QB + SEL_TILE - 1) // SEL_TILE
    kbase = jnp.maximum(t0 + QB - n_win * SEL_TILE, 0)
    for w in range(n_win):
        k0 = pl.multiple_of(kbase + w * SEL_TILE, QB)
        s = lax.dot_general(q, kw_ref[pl.ds(k0, SEL_TILE), :], nt, preferred_element_type=F32)
        kpos = k0 + lane
        valid = (kpos <= tq) & (kpos > tq - C_WINDOW)
        _softmax_step(s, valid, vw_ref[pl.ds(k0, SEL_TILE), :], m_ref, l_ref, acc_ref)
    o_w = acc_ref[...] / l_ref[...]

    gates = jax.nn.sigmoid(gate_ref[...].astype(F32) + bg_ref[...])

    def gate_rows(br):
        return jnp.concatenate([gates[:, br * HG + h:br * HG + h + 1] for h in range(HG)], axis=0)

    out = gate_rows(0) * o_c + gate_rows(1) * o_s + gate_rows(2) * o_w
    o_ref[...] = jnp.concatenate([out[h * QB:(h + 1) * QB] for h in range(HG)], axis=1).astype(o_ref.dtype)


def _nsa_attn(p1, cmp_kv, bg, ovl, expand, bsz, seq):
    QB, HD, G = C_Q_BLOCK, C_HD, C_KV_GROUPS
    nq = seq // QB
    ncmp = seq // C_CMP_STRIDE
    nsel = seq // C_SEL_LEN
    n_tiles = seq // SEL_TILE
    R = QB * C_HG
    qw = C_HG * HD

    def kv_spec(col0):
        return pl.BlockSpec((None, seq, HD), lambda b, g, i: (b, 0, col0 // HD + g))

    return pl.pallas_call(
        _nsa_kernel,
        grid=(bsz, G, nq),
        in_specs=[
            pl.BlockSpec((None, QB, qw), lambda b, g, i: (b, i, g)),
            pl.BlockSpec((None, QB, LANES), lambda b, g, i: (b, i, P1_G // LANES + g)),
            pl.BlockSpec((None, 1, LANES), lambda b, g, i: (g, 0, 0)),
            pl.BlockSpec((None, None, None, ncmp, HD), lambda b, g, i: (0, b, g, 0, 0)),
            pl.BlockSpec((None, None, None, ncmp, HD), lambda b, g, i: (1, b, g, 0, 0)),
            kv_spec(P1_KS), kv_spec(P1_VS), kv_spec(P1_KW), kv_spec(P1_VW),
            pl.BlockSpec((ncmp, nsel), lambda b, g, i: (0, 0)),
            pl.BlockSpec((nsel, seq), lambda b, g, i: (0, 0)),
        ],
        out_specs=pl.BlockSpec((None, QB, qw), lambda b, g, i: (b, i, g)),
        out_shape=jax.ShapeDtypeStruct((bsz, seq, C_HEADS * HD), BF16),
        scratch_shapes=[pltpu.VMEM((n_tiles, QB, SEL_TILE), F32),
                        pltpu.VMEM((R, 1), F32), pltpu.VMEM((R, 1), F32), pltpu.VMEM((R, HD), F32)],
        compiler_params=_params(("parallel", "parallel", "arbitrary")),
        name="nsa_attn",
    )(p1, p1, bg, cmp_kv, cmp_kv, p1, p1, p1, p1, ovl, expand)


def _split_hi_lo(w):
    hi = w.astype(BF16)
    return hi, (w - hi.astype(F32)).astype(BF16)


def _rope_tables(seq):
    half = B_ROPE // 2
    inv = ROPE_THETA ** (-jnp.arange(half, dtype=F32) / half)
    ang = jnp.arange(seq, dtype=F32)[:, None] * inv[None, :]
    cos, sin = jnp.cos(ang), jnp.sin(ang)
    z = jnp.zeros((seq, LANES - B_ROPE), F32)
    zh = jnp.zeros((seq, half), F32)
    return (jnp.concatenate([cos, cos, z], 1), jnp.concatenate([-sin, zh, z], 1),
            jnp.concatenate([zh, sin, z], 1))


def _ffn(x_bf, wg, wu, wd):
    pad = FFN_PAD - FFN_HIDDEN
    wg = jnp.pad(wg, ((0, 0), (0, pad))).astype(BF16)
    wu = jnp.pad(wu, ((0, 0), (0, pad))).astype(BF16)
    wd = jnp.pad(wd, ((0, pad), (0, 0))).astype(BF16)
    h = _swiglu_up(x_bf, wg, wu, tm=1024, tn=512)
    return _matmul_kgrid(h, wd, tm=1024, tn=2048, tk=1024, name="ffn_down")


def _mixer_ab(x2, x2_bf, bsz, seq, w_in, b_ig, b_fg, mlstm_norm, q_norm, kv_norm, w_uq, w_ukv, w_o):
    m = bsz * seq
    wq, wk, wv, wig, wfg, wog, wcq, wckv, wkr = jnp.split(w_in, AB_SPLITS, axis=1)
    w_main = jnp.concatenate([wq, wk, wv, wog, wcq, wckv], 1).astype(BF16)
    w_misc = jnp.concatenate([wkr, wig, wfg, jnp.zeros((D_MODEL, LANES - B_ROPE - 2 * A_HEADS), F32)], 1)
    p0 = _matmul(x2_bf, w_main, BF16, tm=1024, tn=768, name="proj_ab")
    misc = _misc_proj(x2, *_split_hi_lo(w_misc), tm=512)

    L = MLSTM_CHUNK
    gates = misc[:, MISC_IG:MISC_IG + 2 * A_HEADS].reshape(bsz, seq, 2, A_HEADS)
    gates_c = jnp.transpose(gates, (0, 3, 1, 2))
    gates_r = jnp.transpose(gates.reshape(bsz, seq // L, L, 2, A_HEADS), (0, 4, 1, 3, 2))
    bias = jnp.stack([b_ig, b_fg], -1).reshape(A_HEADS, 1, 2)
    h_a = _mlstm(p0.reshape(bsz, seq, P0_N), gates_r, gates_c, bias,
                 mlstm_norm.reshape(A_HEADS, 1, A_DV), bsz, seq)

    wuq = jnp.pad(w_uq.reshape(B_Q_LORA, B_HEADS, B_NOPE + B_ROPE),
                  ((0, 0), (0, 0), (0, 2 * LANES - B_NOPE - B_ROPE))).reshape(B_Q_LORA, -1).astype(BF16)
    wukv = w_ukv.reshape(B_KV_LORA, B_HEADS, B_NOPE + B_DV)
    wuk = wukv[:, :, :B_NOPE].reshape(B_KV_LORA, -1).astype(BF16)
    wuv = wukv[:, :, B_NOPE:].reshape(B_KV_LORA, -1).astype(BF16)
    cos, sina, sinb = _rope_tables(seq)
    qc, kc, vv = _mla_prep(p0, misc, q_norm.reshape(1, -1), kv_norm.reshape(1, -1), wuq, wuk, wuv,
                           cos, sina, sinb, seq, tm=512)
    h_b = _mla_attn(qc.reshape(bsz, seq, -1), kc.reshape(bsz, seq, -1), vv.reshape(bsz, seq, -1),
                    bsz, seq, t=512)
    h = jnp.concatenate([h_a, h_b], -1).reshape(m, D_MODEL)
    return _matmul(h, w_o.astype(BF16), F32, tm=1024, tn=1024, name="out_ab")


def _mixer_c(x2_bf, bsz, seq, w_in, b_gate, pe_k, pe_v, w1_k, w2_k, w1_v, w2_v, w_o):
    m = bsz * seq
    G, HG, HD = C_KV_GROUPS, C_HG, C_HD
    wg = w_in[:, C_SPLITS[-1]:].reshape(D_MODEL, 3, G, HG)
    wg = jnp.transpose(wg, (0, 2, 1, 3)).reshape(D_MODEL, G, 3 * HG)
    wg = jnp.pad(wg, ((0, 0), (0, 0), (0, LANES - 3 * HG))).reshape(D_MODEL, G * LANES)
    w_main = jnp.concatenate([w_in[:, :C_SPLITS[-1]], wg], 1).astype(BF16)
    bg = jnp.transpose(b_gate.reshape(3, G, HG), (1, 0, 2)).reshape(G, 1, 3 * HG)
    bg = jnp.pad(bg, ((0, 0), (0, 0), (0, LANES - 3 * HG)))
    p1 = _matmul(x2_bf, w_main, BF16, tm=1024, tn=768, name="proj_c")
    p1 = p1.reshape(bsz, seq, P1_N)

    nseg = seq // C_CMP_STRIDE
    kv = p1[:, :, P1_KC:P1_KS].reshape(bsz, nseg, C_CMP_STRIDE, 2, G, HD)
    segs = jnp.transpose(kv, (3, 0, 4, 1, 2, 5)).reshape(2, bsz, G, nseg, C_CMP_STRIDE * HD)
    pe = jnp.stack([pe_k.reshape(1, -1), pe_v.reshape(1, -1)])
    w1 = jnp.stack([w1_k, w1_v]).astype(BF16)
    w2 = jnp.stack([w2_k, w2_v]).astype(BF16)
    cmp_kv = _compress(segs, pe, w1, w2)

    nsel = seq // C_SEL_LEN
    cmp_start = np.arange(nseg) * C_CMP_STRIDE
    sel_start = np.arange(nsel) * C_SEL_LEN
    ovl = ((cmp_start[:, None] < sel_start[None, :] + C_SEL_LEN)
           & (cmp_start[:, None] + C_CMP_LEN > sel_start[None, :])).astype(np.float32)
    expand = (np.arange(seq)[None, :] // C_SEL_LEN == np.arange(nsel)[:, None]).astype(np.float32)
    out = _nsa_attn(p1, cmp_kv, bg, jnp.asarray(ovl, BF16), jnp.asarray(expand, BF16), bsz, seq)
    return _matmul(out.reshape(m, D_MODEL), w_o.astype(BF16), F32, tm=1024, tn=1024, name="out_c")


def kernel(x, ab_w_in, ab_b_igate, ab_b_fgate, ab_mlstm_norm, ab_q_norm, ab_kv_norm, ab_w_uq, ab_w_ukv, ab_w_o, c_w_in, c_b_gate, c_pe_k, c_pe_v, c_cmp_w1_k, c_cmp_w2_k, c_cmp_w1_v, c_cmp_w2_v, c_w_o, ffn_w_gate, ffn_w_up, ffn_w_down, ln_mix_g, ln_mix_b, ln_ffn_g, ln_ffn_b):
    bsz, seq, d = x.shape
    m = bsz * seq
    x2 = x.reshape(m, d)
    x2_bf = x2.astype(BF16)
    for layer in range(DEPTH):
        j = layer // 2
        if layer % 2 == 0:
            y = _mixer_ab(x2, x2_bf, bsz, seq, ab_w_in[j], ab_b_igate[j], ab_b_fgate[j], ab_mlstm_norm[j],
                          ab_q_norm[j], ab_kv_norm[j], ab_w_uq[j], ab_w_ukv[j], ab_w_o[j])
        else:
            y = _mixer_c(x2_bf, bsz, seq, c_w_in[j], c_b_gate[j], c_pe_k[j], c_pe_v[j], c_cmp_w1_k[j],
                         c_cmp_w2_k[j], c_cmp_w1_v[j], c_cmp_w2_v[j], c_w_o[j])
        x2, x2_bf = _add_ln(x2, y, ln_mix_g[layer], ln_mix_b[layer])
        y = _ffn(x2_bf, ffn_w_gate[layer], ffn_w_up[layer], ffn_w_down[layer])
        x2, x2_bf = _add_ln(x2, y, ln_ffn_g[layer], ln_ffn_b[layer])
    return x2.reshape(bsz, seq, d)
```

```python
import functools
import math

import numpy as np
import jax
import jax.numpy as jnp
from jax import lax
from jax.experimental import pallas as pl
from jax.experimental.pallas import tpu as pltpu

F32 = jnp.float32
BF16 = jnp.bfloat16

D_MODEL = 4096
DEPTH = 2
DN_ALPHA = (2 * DEPTH) ** 0.25
LN_EPS = 1e-5
RMS_EPS = 1e-6
NEG_INF = -1e30
FORCE_SCORE = 1e9
LOG2E = 1.4426950408889634

A_HEADS, A_DQK, A_DV = 4, 256, 512
A_QK, A_V = A_HEADS * A_DQK, A_HEADS * A_DV
A_GATE_CAP = 15.0
B_HEADS, B_Q_LORA, B_KV_LORA, B_NOPE, B_ROPE, B_DV = 16, 1024, 512, 128, 64, 128
ROPE_THETA = 10000.0
C_HEADS, C_KV_GROUPS, C_HD = 32, 4, 128
C_HG = C_HEADS // C_KV_GROUPS
C_KV = C_KV_GROUPS * C_HD
C_CMP_LEN, C_CMP_STRIDE, C_SEL_LEN, C_N_SEL, C_WINDOW, C_CMP_HIDDEN = 32, 16, 64, 16, 512, 256
C_Q_BLOCK = 64
FFN_HIDDEN = 11008
FFN_PAD = 11264

AB_SPLITS = [int(v) for v in np.cumsum([A_QK, A_QK, A_V, A_HEADS, A_HEADS, A_V, B_Q_LORA, B_KV_LORA])]
C_SPLITS = [int(v) for v in np.cumsum([C_HEADS * C_HD] + [C_KV] * 6)]

LANES = 128
VMEM_LIMIT = 56 * 1024 * 1024

P0_Q, P0_K, P0_V, P0_OG, P0_CQ, P0_CKV, P0_N = 0, 1024, 2048, 4096, 6144, 7168, 7680
MISC_IG, MISC_FG = 64, 68
P1_Q, P1_KC, P1_VC, P1_KS, P1_VS, P1_KW, P1_VW, P1_G, P1_N = 0, 4096, 4608, 5120, 5632, 6144, 6656, 7168, 7680

MLSTM_CHUNK = 256
SEL_TILE = 256
N_SEL_BLOCKS_MAX = 64


def _params(sem):
    return pltpu.CompilerParams(dimension_semantics=sem, vmem_limit_bytes=VMEM_LIMIT)


def _mm_kernel(a_ref, b_ref, o_ref):
    o_ref[...] = jnp.dot(a_ref[...], b_ref[...], preferred_element_type=F32).astype(o_ref.dtype)


def _matmul(a, b, out_dtype, tm, tn, name):
    m, k = a.shape
    n = b.shape[1]
    assert m % tm == 0 and n % tn == 0
    return pl.pallas_call(
        _mm_kernel,
        grid=(m // tm, n // tn),
        in_specs=[pl.BlockSpec((tm, k), lambda i, j: (i, 0)),
                  pl.BlockSpec((k, tn), lambda i, j: (0, j))],
        out_specs=pl.BlockSpec((tm, tn), lambda i, j: (i, j)),
        out_shape=jax.ShapeDtypeStruct((m, n), out_dtype),
        compiler_params=_params(("parallel", "arbitrary")),
        name=name,
    )(a, b)


def _mm_acc_kernel(a_ref, b_ref, o_ref):
    k = pl.program_id(2)

    @pl.when(k == 0)
    def _():
        o_ref[...] = jnp.dot(a_ref[...], b_ref[...], preferred_element_type=F32)

    @pl.when(k > 0)
    def _():
        o_ref[...] += jnp.dot(a_ref[...], b_ref[...], preferred_element_type=F32)


def _matmul_kgrid(a, b, tm, tn, tk, name):
    m, k = a.shape
    n = b.shape[1]
    assert m % tm == 0 and n % tn == 0 and k % tk == 0
    return pl.pallas_call(
        _mm_acc_kernel,
        grid=(m // tm, n // tn, k // tk),
        in_specs=[pl.BlockSpec((tm, tk), lambda i, j, kk: (i, kk)),
                  pl.BlockSpec((tk, tn), lambda i, j, kk: (kk, j))],
        out_specs=pl.BlockSpec((tm, tn), lambda i, j, kk: (i, j)),
        out_shape=jax.ShapeDtypeStruct((m, n), F32),
        compiler_params=_params(("parallel", "arbitrary", "arbitrary")),
        name=name,
    )(a, b)


def _misc_kernel(x_ref, wh_ref, wl_ref, o_ref):
    x = x_ref[...]
    xh = x.astype(BF16)
    xl = (x - xh.astype(F32)).astype(BF16)
    wh = wh_ref[...]
    acc = jnp.dot(xh, wh, preferred_element_type=F32)
    acc += jnp.dot(xl, wh, preferred_element_type=F32)
    acc += jnp.dot(xh, wl_ref[...], preferred_element_type=F32)
    o_ref[...] = acc


def _misc_proj(x, w_hi, w_lo, tm):
    m, k = x.shape
    n = w_hi.shape[1]
    return pl.pallas_call(
        _misc_kernel,
        grid=(m // tm,),
        in_specs=[pl.BlockSpec((tm, k), lambda i: (i, 0)),
                  pl.BlockSpec((k, n), lambda i: (0, 0)),
                  pl.BlockSpec((k, n), lambda i: (0, 0))],
        out_specs=pl.BlockSpec((tm, n), lambda i: (i, 0)),
        out_shape=jax.ShapeDtypeStruct((m, n), F32),
        compiler_params=_params(("parallel",)),
        name="misc_proj",
    )(x, w_hi, w_lo)


def _swiglu_up_kernel(x_ref, wg_ref, wu_ref, o_ref):
    x = x_ref[...]
    g = jnp.dot(x, wg_ref[...], preferred_element_type=F32)
    u = jnp.dot(x, wu_ref[...], preferred_element_type=F32)
    o_ref[...] = (g * jax.nn.sigmoid(g) * u).astype(o_ref.dtype)


def _swiglu_up(x, wg, wu, tm, tn):
    m, k = x.shape
    n = wg.shape[1]
    return pl.pallas_call(
        _swiglu_up_kernel,
        grid=(m // tm, n // tn),
        in_specs=[pl.BlockSpec((tm, k), lambda i, j: (i, 0)),
                  pl.BlockSpec((k, tn), lambda i, j: (0, j)),
                  pl.BlockSpec((k, tn), lambda i, j: (0, j))],
        out_specs=pl.BlockSpec((tm, tn), lambda i, j: (i, j)),
        out_shape=jax.ShapeDtypeStruct((m, n), BF16),
        compiler_params=_params(("parallel", "arbitrary")),
        name="swiglu_up",
    )(x, wg, wu)


def _add_ln_kernel(x_ref, y_ref, g_ref, b_ref, o_ref, ob_ref):
    z = DN_ALPHA * x_ref[...] + y_ref[...]
    mu = jnp.mean(z, axis=-1, keepdims=True)
    zc = z - mu
    var = jnp.mean(zc * zc, axis=-1, keepdims=True)
    out = zc * lax.rsqrt(var + LN_EPS) * g_ref[...] + b_ref[...]
    o_ref[...] = out
    ob_ref[...] = out.astype(BF16)


def _add_ln(x, y, g, b, tm=256):
    m, d = x.shape
    row = pl.BlockSpec((tm, d), lambda i: (i, 0))
    vec = pl.BlockSpec((1, d), lambda i: (0, 0))
    return pl.pallas_call(
        _add_ln_kernel,
        grid=(m // tm,),
        in_specs=[row, row, vec, vec],
        out_specs=[row, row],
        out_shape=[jax.ShapeDtypeStruct((m, d), F32), jax.ShapeDtypeStruct((m, d), BF16)],
        compiler_params=_params(("parallel",)),
        name="add_ln",
    )(x, y, g.reshape(1, d), b.reshape(1, d))


def _soft_cap(z):
    return A_GATE_CAP * jnp.tanh(z / A_GATE_CAP)


def _log_sigmoid(z):
    return jnp.minimum(z, 0.0) - jnp.log1p(jnp.exp(-jnp.abs(z)))


def _mlstm_kernel(q_ref, k_ref, v_ref, og_ref, gr_ref, gc_ref, bias_ref, norm_ref, o_ref,
                  c_ref, n_ref, m_ref):
    L = q_ref.shape[0]
    c = pl.program_id(2)

    @pl.when(c == 0)
    def _():
        c_ref[...] = jnp.zeros_like(c_ref)
        n_ref[...] = jnp.zeros_like(n_ref)
        m_ref[...] = jnp.zeros_like(m_ref)

    bias = bias_ref[...]
    gr = gr_ref[...]
    gc = gc_ref[...]
    li_r = _soft_cap(gr[0:1, :] + bias[:, 0:1])
    lf_r = _log_sigmoid(_soft_cap(gr[1:2, :] + bias[:, 1:2]))
    li_c = _soft_cap(gc[:, 0:1] + bias[:, 0:1])
    lf_c = _log_sigmoid(_soft_cap(gc[:, 1:2] + bias[:, 1:2]))

    t_idx = lax.broadcasted_iota(jnp.int32, (L, L), 0)
    s_idx = lax.broadcasted_iota(jnp.int32, (L, L), 1)
    causal = s_idx <= t_idx
    b_c = jnp.sum(jnp.where(causal, lf_r, 0.0), axis=1, keepdims=True)
    b_r = jnp.sum(jnp.where(t_idx <= s_idx, lf_c, 0.0), axis=0, keepdims=True)
    g_tot = jnp.sum(lf_r, axis=1, keepdims=True)

    m_prev = m_ref[...]
    dmat = jnp.where(causal, b_c - b_r + li_r, NEG_INF)
    m_inter = b_c + m_prev
    m_t = jnp.maximum(m_inter, jnp.max(dmat, axis=1, keepdims=True))
    w_inter = jnp.exp(m_inter - m_t)
    pmat = jnp.exp(dmat - m_t)

    q = q_ref[...]
    k = k_ref[...] * (A_DQK ** -0.5)
    v = v_ref[...]
    qk = lax.dot_general(q, k, (((1,), (1,)), ((), ())), preferred_element_type=F32)
    s = qk * pmat
    num = w_inter * jnp.dot(q, c_ref[...].astype(BF16), preferred_element_type=F32)
    num += jnp.dot(s.astype(BF16), v, preferred_element_type=F32)
    den = w_inter * jnp.sum(q.astype(F32) * n_ref[...], axis=1, keepdims=True)
    den += jnp.sum(s, axis=1, keepdims=True)
    h = num / jnp.maximum(jnp.abs(den), jnp.exp(-m_t))

    hn = h * lax.rsqrt(jnp.mean(h * h, axis=-1, keepdims=True) + RMS_EPS) * norm_ref[...]
    o_ref[...] = (hn * jax.nn.sigmoid(og_ref[...].astype(F32))).astype(o_ref.dtype)

    ws_c = g_tot - b_c + li_c
    ws_r = g_tot - b_r + li_r
    m_new = jnp.maximum(g_tot + m_prev, jnp.max(ws_r, axis=1, keepdims=True))
    decay = jnp.exp(g_tot + m_prev - m_new)
    kw = k.astype(F32) * jnp.exp(ws_c - m_new)
    c_ref[...] = decay * c_ref[...] + lax.dot_general(
        kw.astype(BF16), v, (((0,), (0,)), ((), ())), preferred_element_type=F32)
    n_ref[...] = decay * n_ref[...] + jnp.sum(kw, axis=0, keepdims=True)
    m_ref[...] = m_new


def _mlstm(p0, gates_r, gates_c, bias, norm, bsz, seq):
    L = MLSTM_CHUNK
    nc = seq // L
    qb, kb, vb, ogb = P0_Q // A_DQK, P0_K // A_DQK, P0_V // A_DV, P0_OG // A_DV
    return pl.pallas_call(
        _mlstm_kernel,
        grid=(bsz, A_HEADS, nc),
        in_specs=[
            pl.BlockSpec((None, L, A_DQK), lambda b, h, c: (b, c, qb + h)),
            pl.BlockSpec((None, L, A_DQK), lambda b, h, c: (b, c, kb + h)),
            pl.BlockSpec((None, L, A_DV), lambda b, h, c: (b, c, vb + h)),
            pl.BlockSpec((None, L, A_DV), lambda b, h, c: (b, c, ogb + h)),
            pl.BlockSpec((None, None, None, 2, L), lambda b, h, c: (b, h, c, 0, 0)),
            pl.BlockSpec((None, None, L, 2), lambda b, h, c: (b, h, c, 0)),
            pl.BlockSpec((None, 1, 2), lambda b, h, c: (h, 0, 0)),
            pl.BlockSpec((None, 1, A_DV), lambda b, h, c: (h, 0, 0)),
        ],
        out_specs=pl.BlockSpec((None, L, A_DV), lambda b, h, c: (b, c, h)),
        out_shape=jax.ShapeDtypeStruct((bsz, seq, A_V), BF16),
        scratch_shapes=[pltpu.VMEM((A_DQK, A_DV), F32), pltpu.VMEM((1, A_DQK), F32), pltpu.VMEM((1, 1), F32)],
        compiler_params=_params(("parallel", "parallel", "arbitrary")),
        name="mlstm",
    )(p0, p0, p0, p0, gates_r, gates_c, bias, norm)


def _rms(x, g):
    return x * lax.rsqrt(jnp.mean(x * x, axis=-1, keepdims=True) + RMS_EPS) * g


def _rope_tile(x, cos, sina, sinb):
    return x * cos + pltpu.roll(x, 96, 1) * sina + pltpu.roll(x, 32, 1) * sinb


def _mla_prep_kernel(cq_ref, ckv_ref, kr_ref, qg_ref, kvg_ref, wuq_ref, wuk_ref, wuv_ref,
                     cos_ref, sina_ref, sinb_ref, q_ref, k_ref, v_ref):
    scale = (B_NOPE + B_ROPE) ** -0.5 * LOG2E
    cos, sina, sinb = cos_ref[...], sina_ref[...], sinb_ref[...]
    cqn = _rms(cq_ref[...].astype(F32), qg_ref[...]).astype(BF16)
    qb = jnp.dot(cqn, wuq_ref[...], preferred_element_type=F32)
    ckvn = _rms(ckv_ref[...].astype(F32), kvg_ref[...]).astype(BF16)
    kup = jnp.dot(ckvn, wuk_ref[...], preferred_element_type=F32)
    v_ref[...] = jnp.dot(ckvn, wuv_ref[...], preferred_element_type=F32).astype(BF16)
    krope = _rope_tile(kr_ref[...], cos, sina, sinb).astype(BF16)
    for h in range(B_HEADS):
        c0 = 2 * LANES * h
        q_ref[:, c0:c0 + LANES] = (qb[:, c0:c0 + LANES] * scale).astype(BF16)
        q_ref[:, c0 + LANES:c0 + 2 * LANES] = (
            _rope_tile(qb[:, c0 + LANES:c0 + 2 * LANES], cos, sina, sinb) * scale).astype(BF16)
        k_ref[:, c0:c0 + LANES] = kup[:, LANES * h:LANES * (h + 1)].astype(BF16)
        k_ref[:, c0 + LANES:c0 + 2 * LANES] = krope


def _mla_prep(p0, misc, qg, kvg, wuq, wuk, wuv, cos, sina, sinb, seq, tm):
    m = p0.shape[0]
    nrope = seq // tm
    hq = B_HEADS * 2 * LANES
    hv = B_HEADS * B_DV
    full = lambda shape: pl.BlockSpec(shape, lambda i: (0, 0))
    tab = pl.BlockSpec((tm, LANES), lambda i: (i % nrope, 0))
    return pl.pallas_call(
        _mla_prep_kernel,
        grid=(m // tm,),
        in_specs=[
            pl.BlockSpec((tm, B_Q_LORA), lambda i: (i, P0_CQ // B_Q_LORA)),
            pl.BlockSpec((tm, B_KV_LORA), lambda i: (i, P0_CKV // B_KV_LORA)),
            pl.BlockSpec((tm, LANES), lambda i: (i, 0)),
            full((1, B_Q_LORA)), full((1, B_KV_LORA)),
            full((B_Q_LORA, hq)), full((B_KV_LORA, hv)), full((B_KV_LORA, hv)),
            tab, tab, tab,
        ],
        out_specs=[pl.BlockSpec((tm, hq), lambda i: (i, 0)),
                   pl.BlockSpec((tm, hq), lambda i: (i, 0)),
                   pl.BlockSpec((tm, hv), lambda i: (i, 0))],
        out_shape=[jax.ShapeDtypeStruct((m, hq), BF16), jax.ShapeDtypeStruct((m, hq), BF16),
                   jax.ShapeDtypeStruct((m, hv), BF16)],
        compiler_params=_params(("parallel",)),
        name="mla_prep",
    )(p0, p0, misc, qg, kvg, wuq, wuk, wuv, cos, sina, sinb)


def _lane_fold(x, op):
    out = x[:, :LANES]
    for c in range(1, x.shape[1] // LANES):
        out = op(out, x[:, c * LANES:(c + 1) * LANES])
    return out


def _attn_reset(mrun_ref, acc_ref):
    mrun_ref[...] = jnp.full_like(mrun_ref, NEG_INF)
    acc_ref[...] = jnp.zeros_like(acc_ref)


def _score_tile(j, s, s_ref, mrun_ref):
    s_ref[j] = s
    mrun_ref[...] = jnp.maximum(mrun_ref[...], _lane_fold(s, jnp.maximum))


def _rowmax_to_lanes(mrun_ref):
    m = jnp.max(mrun_ref[...], axis=1, keepdims=True)
    mrun_ref[...] = jnp.broadcast_to(m, mrun_ref.shape)


def _prob_tile(j, v, s_ref, mrun_ref, acc_ref):
    s = s_ref[j]
    mb = mrun_ref[...]
    p = jnp.concatenate([jnp.exp2(s[:, c * LANES:(c + 1) * LANES] - mb) for c in range(s.shape[1] // LANES)],
                        axis=1).astype(BF16)
    v_ones = jnp.concatenate([v, jnp.ones((v.shape[0], LANES), BF16)], axis=1)
    acc_ref[...] += jnp.dot(p, v_ones, preferred_element_type=F32)


def _attn_result(acc_ref, hd):
    acc = acc_ref[...]
    return acc[:, :hd] / acc[:, hd:], acc[:, hd:]


def _for_tiles(n, body):
    if isinstance(n, int):
        for j in range(n):
            body(j)
        return

    def pair(t, c):
        body(2 * t)
        body(2 * t + 1)
        return c

    lax.fori_loop(0, n // 2, pair, 0)

    @pl.when(n % 2 == 1)
    def _():
        body(n - 1)


_NT = (((1,), (1,)), ((), ()))


def _mla_attn_kernel(q_ref, k_ref, v_ref, o_ref, s_ref, mrun_ref, acc_ref):
    t = q_ref.shape[0]
    i = pl.program_id(2)
    q = q_ref[...]
    _attn_reset(mrun_ref, acc_ref)

    def scores(j):
        k0 = pl.multiple_of(j * t, t)
        return lax.dot_general(q, k_ref[pl.ds(k0, t), :], _NT, preferred_element_type=F32)

    _for_tiles(i, lambda j: _score_tile(j, scores(j), s_ref, mrun_ref))
    rows = lax.broadcasted_iota(jnp.int32, (t, t), 0)
    cols = lax.broadcasted_iota(jnp.int32, (t, t), 1)
    _score_tile(i, jnp.where(cols <= rows, scores(i), NEG_INF), s_ref, mrun_ref)
    _rowmax_to_lanes(mrun_ref)
    _for_tiles(i + 1, lambda j: _prob_tile(j, v_ref[pl.ds(pl.multiple_of(j * t, t), t), :],
                                           s_ref, mrun_ref, acc_ref))
    o_ref[...] = _attn_result(acc_ref, B_DV)[0].astype(o_ref.dtype)


def _mla_attn(qc, kc, vv, bsz, seq, t):
    n = seq // t
    return pl.pallas_call(
        _mla_attn_kernel,
        grid=(bsz, B_HEADS, n),
        in_specs=[pl.BlockSpec((None, t, 2 * LANES), lambda b, h, i: (b, i, h)),
                  pl.BlockSpec((None, seq, 2 * LANES), lambda b, h, i: (b, 0, h)),
                  pl.BlockSpec((None, seq, B_DV), lambda b, h, i: (b, 0, h))],
        out_specs=pl.BlockSpec((None, t, B_DV), lambda b, h, i: (b, i, h)),
        out_shape=jax.ShapeDtypeStruct((bsz, seq, B_HEADS * B_DV), BF16),
        scratch_shapes=[pltpu.VMEM((n, t, t), F32), pltpu.VMEM((t, LANES), F32),
                        pltpu.VMEM((t, B_DV + LANES), F32)],
        compiler_params=_params(("parallel", "parallel", "arbitrary")),
        name="mla_attn",
    )(qc, kc, vv)


def _gelu_tanh(x):
    return 0.5 * x * (1.0 + jnp.tanh(math.sqrt(2.0 / math.pi) * (x + 0.044715 * (x * x * x))))


def _compress_kernel(seg_ref, pe_ref, w1_ref, w2_ref, o_ref):
    half = seg_ref.shape[1]
    nseg = seg_ref.shape[0]
    seg = seg_ref[...].astype(F32)
    pe = pe_ref[...]
    top = (seg + pe[:, :half]).astype(BF16)
    bot = (seg + pe[:, half:]).astype(BF16)
    u = jnp.dot(top, w1_ref[:half, :], preferred_element_type=F32)
    w = jnp.dot(bot, w1_ref[half:, :], preferred_element_type=F32)
    hid = u + pltpu.roll(w, nseg - 1, 0)
    o_ref[...] = jnp.dot(_gelu_tanh(hid).astype(BF16), w2_ref[...], preferred_element_type=F32).astype(BF16)


def _compress(segs, pe, w1, w2):
    _, bsz, g, nseg, width = segs.shape
    return pl.pallas_call(
        _compress_kernel,
        grid=(2, bsz, g),
        in_specs=[pl.BlockSpec((None, None, None, nseg, width), lambda t, b, gg: (t, b, gg, 0, 0)),
                  pl.BlockSpec((None, 1, 2 * width), lambda t, b, gg: (t, 0, 0)),
                  pl.BlockSpec((None, 2 * width, C_CMP_HIDDEN), lambda t, b, gg: (t, 0, 0)),
                  pl.BlockSpec((None, C_CMP_HIDDEN, C_HD), lambda t, b, gg: (t, 0, 0))],
        out_specs=pl.BlockSpec((None, None, None, nseg, C_HD), lambda t, b, gg: (t, b, gg, 0, 0)),
        out_shape=jax.ShapeDtypeStruct((2, bsz, g, nseg, C_HD), BF16),
        compiler_params=_params(("parallel", "parallel", "parallel")),
        name="nsa_compress",
    )(segs, pe, w1, w2)


def _split2(x):
    hi = x.astype(BF16)
    return hi, (x - hi.astype(F32)).astype(BF16)


def _topk_blocks(imp_t2, i):
    nsel, width = imp_t2.shape
    half_w, half_n = width // 2, nsel // 2
    sblk = lax.broadcasted_iota(jnp.int32, imp_t2.shape, 0)
    upper = lax.broadcasted_iota(jnp.int32, imp_t2.shape, 1) >= half_w
    forced = (sblk == 0) | (sblk == i) | (sblk == i - 1)
    key = lax.bitcast_convert_type(jnp.where(forced, FORCE_SCORE, jnp.abs(imp_t2)), jnp.int32)
    key = jnp.where(sblk <= i, key, -(2 ** 30))
    key_m1 = key - 1
    sblk_adj = sblk - jnp.where(upper, half_n, 0)
    rank = jnp.zeros(imp_t2.shape, jnp.int32)
    for t in range(half_n):
        cand = jnp.where(upper[0:1], key[t + half_n:t + half_n + 1, :], key[t:t + 1, :])
        thr = jnp.where(sblk_adj > t, key_m1, key)
        rank = rank + jnp.where(cand > thr, 1, 0)
    rank = rank + pltpu.roll(rank, half_w, 1)
    return rank < C_N_SEL


def _nsa_branch(q, k_ref, v_ref, n_tiles, k0_fn, bias_ref, s_ref, mrun_ref, acc_ref):
    R, QB = q.shape[0], bias_ref.shape[1]
    _attn_reset(mrun_ref, acc_ref)

    def p1(j):
        s = lax.dot_general(q, k_ref[pl.ds(k0_fn(j), SEL_TILE), :], _NT, preferred_element_type=F32)
        s = (s.reshape(R // QB, QB, SEL_TILE) + bias_ref[j][None]).reshape(R, SEL_TILE)
        _score_tile(j, s, s_ref, mrun_ref)

    _for_tiles(n_tiles, p1)
    _rowmax_to_lanes(mrun_ref)
    _for_tiles(n_tiles, lambda j: _prob_tile(j, v_ref[pl.ds(k0_fn(j), SEL_TILE), :], s_ref, mrun_ref, acc_ref))
    return _attn_result(acc_ref, C_HD)[0]


def _nsa_kernel(q_ref, gate_ref, bg_ref, kcmp_ref, vcmp_ref, ks_ref, vs_ref, kw_ref, vw_ref,
                cext_ref, exp_ref, gsel_ref, o_ref, s_ref, bias_ref, wbias_ref, gexp_ref, mrun_ref, acc_ref,
                out_ref):
    QB, HG, HD = C_Q_BLOCK, C_HG, C_HD
    R = QB * HG
    i = pl.program_id(2)
    t0 = i * QB
    scale = HD ** -0.5 * LOG2E

    q_all = q_ref[...]
    q = jnp.concatenate([q_all[:, h * HD:(h + 1) * HD] for h in range(HG)], axis=0)
    q = (q.astype(F32) * scale).astype(BF16)
    tq = t0 + lax.broadcasted_iota(jnp.int32, (QB, 1), 0)
    lane = lax.broadcasted_iota(jnp.int32, (QB, SEL_TILE), 1)

    g_hi, g_lo = _split2(jax.nn.sigmoid(gate_ref[...].astype(F32) + bg_ref[...]))
    gexp_ref[...] = (jnp.dot(g_hi, gsel_ref[...], preferred_element_type=F32)
                     + jnp.dot(g_lo, gsel_ref[...], preferred_element_type=F32))

    def gate_rows(br):
        return jnp.concatenate([gexp_ref[:, (br * HG + h) * LANES:(br * HG + h + 1) * LANES] for h in range(HG)],
                               axis=0)

    ncmp = kcmp_ref.shape[0]
    vis = lax.broadcasted_iota(jnp.int32, (QB, ncmp), 1) * C_CMP_STRIDE + (C_CMP_LEN - 1) <= tq
    s_c = lax.dot_general(q, kcmp_ref[...], _NT, preferred_element_type=F32).reshape(HG, QB, ncmp)
    s_c = s_c + jnp.where(vis, 0.0, NEG_INF)[None]
    e = jnp.exp2(s_c - jnp.max(s_c, axis=2, keepdims=True)) * jnp.where(vis, 1.0, 0.0)[None]
    e_hi, e_lo = _split2(e.reshape(R, ncmp))
    cext = cext_ref[...]
    r_hi = jnp.dot(e_hi, jnp.concatenate([vcmp_ref[...], cext], axis=1), preferred_element_type=F32)
    r_lo = jnp.dot(e_lo, cext, preferred_element_type=F32)
    l_c = r_hi[:, HD:HD + LANES] + r_lo[:, :LANES]
    inv = 1.0 / jnp.where(l_c > 0.0, l_c, 1.0)
    out_ref[...] = gate_rows(0) * (r_hi[:, :HD] * inv)

    imp = jnp.sum(((r_hi[:, HD + LANES:] + r_lo[:, LANES:]) * inv).reshape(HG, QB, LANES), axis=0)
    imp_t = jnp.concatenate([imp, jnp.zeros((LANES - QB, LANES), F32)], axis=0).T
    nsel = exp_ref.shape[0]
    imp_t = imp_t[:nsel]
    sel = _topk_blocks(imp_t + pltpu.roll(imp_t, QB, 1), i)
    drop = jnp.where(sel[:, :QB], 0.0, -1.0).astype(BF16)
    bias = lax.dot_general(drop, exp_ref[...], (((0,), (0,)), ((), ())), preferred_element_type=F32)
    for j in range(bias_ref.shape[0]):
        bias_ref[j] = bias[:, j * SEL_TILE:(j + 1) * SEL_TILE]
    jd = (t0 + QB - 1) // SEL_TILE
    bias_ref[jd] = jnp.where(jd * SEL_TILE + lane <= tq, bias_ref[jd], NEG_INF)

    o_s = _nsa_branch(q, ks_ref, vs_ref, jd + 1, lambda j: pl.multiple_of(j * SEL_TILE, SEL_TILE),
                      bias_ref, s_ref, mrun_ref, acc_ref)
    out_ref[...] += gate_rows(1) * o_s

    n_win = wbias_ref.shape[0]
    kbase = jnp.maximum(t0 + QB - n_win * SEL_TILE, 0)
    for w in range(n_win):
        kpos = kbase + w * SEL_TILE + lane
        wbias_ref[w] = jnp.where((kpos <= tq) & (kpos > tq - C_WINDOW), 0.0, NEG_INF)
    o_w = _nsa_branch(q, kw_ref, vw_ref, n_win, lambda j: pl.multiple_of(kbase + j * SEL_TILE, QB),
                      wbias_ref, s_ref, mrun_ref, acc_ref)
    out = out_ref[...] + gate_rows(2) * o_w
    o_ref[...] = jnp.concatenate([out[h * QB:(h + 1) * QB] for h in range(HG)], axis=1).astype(o_ref.dtype)


def _nsa_attn(p1, cmp_kv, bg, bsz, seq):
    QB, HD, G = C_Q_BLOCK, C_HD, C_KV_GROUPS
    nq = seq // QB
    ncmp = seq // C_CMP_STRIDE
    nsel = seq // C_SEL_LEN
    n_tiles = seq // SEL_TILE
    n_win = (C_WINDOW + QB + SEL_TILE - 1) // SEL_TILE
    R = QB * C_HG
    qw = C_HG * HD
    n_gate = 3 * C_HG
    assert nsel <= LANES and n_gate <= LANES

    cmp_start = np.arange(ncmp) * C_CMP_STRIDE
    sel_start = np.arange(nsel) * C_SEL_LEN
    ovl = ((cmp_start[:, None] < sel_start[None, :] + C_SEL_LEN)
           & (cmp_start[:, None] + C_CMP_LEN > sel_start[None, :])).astype(np.float32)
    cext = np.concatenate([np.ones((ncmp, LANES), np.float32), ovl, np.zeros((ncmp, LANES - nsel), np.float32)], 1)
    expand = (np.arange(seq)[None, :] // C_SEL_LEN == np.arange(nsel)[:, None]).astype(np.float32) * -NEG_INF
    gsel = (np.arange(n_gate * LANES)[None, :] // LANES == np.arange(LANES)[:, None]).astype(np.float32)
    cext, expand, gsel = (jnp.asarray(a, BF16) for a in (cext, expand, gsel))

    def kv_spec(col0):
        return pl.BlockSpec((None, seq, HD), lambda b, g, i: (b, 0, col0 // HD + g))

    return pl.pallas_call(
        _nsa_kernel,
        grid=(bsz, G, nq),
        in_specs=[
            pl.BlockSpec((None, QB, qw), lambda b, g, i: (b, i, g)),
            pl.BlockSpec((None, QB, LANES), lambda b, g, i: (b, i, P1_G // LANES + g)),
            pl.BlockSpec((None, 1, LANES), lambda b, g, i: (g, 0, 0)),
            pl.BlockSpec((None, None, None, ncmp, HD), lambda b, g, i: (0, b, g, 0, 0)),
            pl.BlockSpec((None, None, None, ncmp, HD), lambda b, g, i: (1, b, g, 0, 0)),
            kv_spec(P1_KS), kv_spec(P1_VS), kv_spec(P1_KW), kv_spec(P1_VW),
            pl.BlockSpec((ncmp, 2 * LANES), lambda b, g, i: (0, 0)),
            pl.BlockSpec((nsel, seq), lambda b, g, i: (0, 0)),
            pl.BlockSpec((LANES, n_gate * LANES), lambda b, g, i: (0, 0)),
        ],
        out_specs=pl.BlockSpec((None, QB, qw), lambda b, g, i: (b, i, g)),
        out_shape=jax.ShapeDtypeStruct((bsz, seq, C_HEADS * HD), BF16),
        scratch_shapes=[pltpu.VMEM((n_tiles, R, SEL_TILE), F32),
                        pltpu.VMEM((n_tiles, QB, SEL_TILE), F32),
                        pltpu.VMEM((n_win, QB, SEL_TILE), F32),
                        pltpu.VMEM((QB, n_gate * LANES), F32),
                        pltpu.VMEM((R, LANES), F32),
                        pltpu.VMEM((R, HD + LANES), F32),
                        pltpu.VMEM((R, HD), F32)],
        compiler_params=_params(("parallel", "parallel", "arbitrary")),
        name="nsa_attn",
    )(p1, p1, bg, cmp_kv, cmp_kv, p1, p1, p1, p1, cext, expand, gsel)


def _split_hi_lo(w):
    hi = w.astype(BF16)
    return hi, (w - hi.astype(F32)).astype(BF16)


def _rope_tables(seq):
    half = B_ROPE // 2
    inv = ROPE_THETA ** (-jnp.arange(half, dtype=F32) / half)
    ang = jnp.arange(seq, dtype=F32)[:, None] * inv[None, :]
    cos, sin = jnp.cos(ang), jnp.sin(ang)
    z = jnp.zeros((seq, LANES - B_ROPE), F32)
    zh = jnp.zeros((seq, half), F32)
    return (jnp.concatenate([cos, cos, z], 1), jnp.concatenate([-sin, zh, z], 1),
            jnp.concatenate([zh, sin, z], 1))


def _ffn(x_bf, wg, wu, wd):
    pad = FFN_PAD - FFN_HIDDEN
    wg = jnp.pad(wg, ((0, 0), (0, pad))).astype(BF16)
    wu = jnp.pad(wu, ((0, 0), (0, pad))).astype(BF16)
    wd = jnp.pad(wd, ((0, pad), (0, 0))).astype(BF16)
    h = _swiglu_up(x_bf, wg, wu, tm=1024, tn=512)
    return _matmul_kgrid(h, wd, tm=1024, tn=2048, tk=1024, name="ffn_down")


def _mixer_ab(x2, x2_bf, bsz, seq, w_in, b_ig, b_fg, mlstm_norm, q_norm, kv_norm, w_uq, w_ukv, w_o):
    m = bsz * seq
    wq, wk, wv, wig, wfg, wog, wcq, wckv, wkr = jnp.split(w_in, AB_SPLITS, axis=1)
    w_main = jnp.concatenate([wq, wk, wv, wog, wcq, wckv], 1).astype(BF16)
    w_misc = jnp.concatenate([wkr, wig, wfg, jnp.zeros((D_MODEL, LANES - B_ROPE - 2 * A_HEADS), F32)], 1)
    p0 = _matmul(x2_bf, w_main, BF16, tm=1024, tn=768, name="proj_ab")
    misc = _misc_proj(x2, *_split_hi_lo(w_misc), tm=512)

    L = MLSTM_CHUNK
    gates = misc[:, MISC_IG:MISC_IG + 2 * A_HEADS].reshape(bsz, seq, 2, A_HEADS)
    gates_c = jnp.transpose(gates, (0, 3, 1, 2))
    gates_r = jnp.transpose(gates.reshape(bsz, seq // L, L, 2, A_HEADS), (0, 4, 1, 3, 2))
    bias = jnp.stack([b_ig, b_fg], -1).reshape(A_HEADS, 1, 2)
    h_a = _mlstm(p0.reshape(bsz, seq, P0_N), gates_r, gates_c, bias,
                 mlstm_norm.reshape(A_HEADS, 1, A_DV), bsz, seq)

    wuq = jnp.pad(w_uq.reshape(B_Q_LORA, B_HEADS, B_NOPE + B_ROPE),
                  ((0, 0), (0, 0), (0, 2 * LANES - B_NOPE - B_ROPE))).reshape(B_Q_LORA, -1).astype(BF16)
    wukv = w_ukv.reshape(B_KV_LORA, B_HEADS, B_NOPE + B_DV)
    wuk = wukv[:, :, :B_NOPE].reshape(B_KV_LORA, -1).astype(BF16)
    wuv = wukv[:, :, B_NOPE:].reshape(B_KV_LORA, -1).astype(BF16)
    cos, sina, sinb = _rope_tables(seq)
    qc, kc, vv = _mla_prep(p0, misc, q_norm.reshape(1, -1), kv_norm.reshape(1, -1), wuq, wuk, wuv,
                           cos, sina, sinb, seq, tm=512)
    h_b = _mla_attn(qc.reshape(bsz, seq, -1), kc.reshape(bsz, seq, -1), vv.reshape(bsz, seq, -1),
                    bsz, seq, t=512)
    h = jnp.concatenate([h_a, h_b], -1).reshape(m, D_MODEL)
    return _matmul(h, w_o.astype(BF16), F32, tm=1024, tn=1024, name="out_ab")


def _mixer_c(x2_bf, bsz, seq, w_in, b_gate, pe_k, pe_v, w1_k, w2_k, w1_v, w2_v, w_o):
    m = bsz * seq
    G, HG, HD = C_KV_GROUPS, C_HG, C_HD
    wg = w_in[:, C_SPLITS[-1]:].reshape(D_MODEL, 3, G, HG)
    wg = jnp.transpose(wg, (0, 2, 1, 3)).reshape(D_MODEL, G, 3 * HG)
    wg = jnp.pad(wg, ((0, 0), (0, 0), (0, LANES - 3 * HG))).reshape(D_MODEL, G * LANES)
    w_main = jnp.concatenate([w_in[:, :C_SPLITS[-1]], wg], 1).astype(BF16)
    bg = jnp.transpose(b_gate.reshape(3, G, HG), (1, 0, 2)).reshape(G, 1, 3 * HG)
    bg = jnp.pad(bg, ((0, 0), (0, 0), (0, LANES - 3 * HG)))
    p1 = _matmul(x2_bf, w_main, BF16, tm=1024, tn=768, name="proj_c")
    p1 = p1.reshape(bsz, seq, P1_N)

    nseg = seq // C_CMP_STRIDE
    kv = p1[:, :, P1_KC:P1_KS].reshape(bsz, nseg, C_CMP_STRIDE, 2, G, HD)
    segs = jnp.transpose(kv, (3, 0, 4, 1, 2, 5)).reshape(2, bsz, G, nseg, C_CMP_STRIDE * HD)
    pe = jnp.stack([pe_k.reshape(1, -1), pe_v.reshape(1, -1)])
    w1 = jnp.stack([w1_k, w1_v]).astype(BF16)
    w2 = jnp.stack([w2_k, w2_v]).astype(BF16)
    cmp_kv = _compress(segs, pe, w1, w2)

    out = _nsa_attn(p1, cmp_kv, bg, bsz, seq)
    return _matmul(out.reshape(m, D_MODEL), w_o.astype(BF16), F32, tm=1024, tn=1024, name="out_c")


def kernel(x, ab_w_in, ab_b_igate, ab_b_fgate, ab_mlstm_norm, ab_q_norm, ab_kv_norm, ab_w_uq, ab_w_ukv, ab_w_o, c_w_in, c_b_gate, c_pe_k, c_pe_v, c_cmp_w1_k, c_cmp_w2_k, c_cmp_w1_v, c_cmp_w2_v, c_w_o, ffn_w_gate, ffn_w_up, ffn_w_down, ln_mix_g, ln_mix_b, ln_ffn_g, ln_ffn_b):
    bsz, seq, d = x.shape
    m = bsz * seq
    x2 = x.reshape(m, d)
    x2_bf = x2.astype(BF16)
    for layer in range(DEPTH):
        j = layer // 2
        if layer % 2 == 0:
            y = _mixer_ab(x2, x2_bf, bsz, seq, ab_w_in[j], ab_b_igate[j], ab_b_fgate[j], ab_mlstm_norm[j],
                          ab_q_norm[j], ab_kv_norm[j], ab_w_uq[j], ab_w_ukv[j], ab_w_o[j])
        else:
            y = _mixer_c(x2_bf, bsz, seq, c_w_in[j], c_b_gate[j], c_pe_k[j], c_pe_v[j], c_cmp_w1_k[j],
                         c_cmp_w2_k[j], c_cmp_w1_v[j], c_cmp_w2_v[j], c_w_o[j])
        x2, x2_bf = _add_ln(x2, y, ln_mix_g[layer], ln_mix_b[layer])
        y = _ffn(x2_bf, ffn_w_gate[layer], ffn_w_up[layer], ffn_w_down[layer])
        x2, x2_bf = _add_ln(x2, y, ln_ffn_g[layer], ln_ffn_b[layer])
    return x2.reshape(bsz, seq, d)
```

```python
import functools
import math

import numpy as np
import jax
import jax.numpy as jnp
from jax import lax
from jax.experimental import pallas as pl
from jax.experimental.pallas import tpu as pltpu

F32 = jnp.float32
BF16 = jnp.bfloat16

D_MODEL = 4096
DEPTH = 2
DN_ALPHA = (2 * DEPTH) ** 0.25
LN_EPS = 1e-5
RMS_EPS = 1e-6
NEG_INF = -1e30
FORCE_SCORE = 1e9
LOG2E = 1.4426950408889634

A_HEADS, A_DQK, A_DV = 4, 256, 512
A_QK, A_V = A_HEADS * A_DQK, A_HEADS * A_DV
A_GATE_CAP = 15.0
B_HEADS, B_Q_LORA, B_KV_LORA, B_NOPE, B_ROPE, B_DV = 16, 1024, 512, 128, 64, 128
ROPE_THETA = 10000.0
C_HEADS, C_KV_GROUPS, C_HD = 32, 4, 128
C_HG = C_HEADS // C_KV_GROUPS
C_KV = C_KV_GROUPS * C_HD
C_CMP_LEN, C_CMP_STRIDE, C_SEL_LEN, C_N_SEL, C_WINDOW, C_CMP_HIDDEN = 32, 16, 64, 16, 512, 256
C_Q_BLOCK = 64
FFN_HIDDEN = 11008
FFN_PAD = 11264

AB_SPLITS = [int(v) for v in np.cumsum([A_QK, A_QK, A_V, A_HEADS, A_HEADS, A_V, B_Q_LORA, B_KV_LORA])]
C_SPLITS = [int(v) for v in np.cumsum([C_HEADS * C_HD] + [C_KV] * 6)]

LANES = 128
VMEM_LIMIT = 56 * 1024 * 1024

P0_Q, P0_K, P0_V, P0_OG, P0_CQ, P0_CKV, P0_N = 0, 1024, 2048, 4096, 6144, 7168, 7680
MISC_IG, MISC_FG = 64, 68
P1_Q, P1_KC, P1_VC, P1_KS, P1_VS, P1_KW, P1_VW, P1_G, P1_N = 0, 4096, 4608, 5120, 5632, 6144, 6656, 7168, 7680

MLSTM_CHUNK = 256
SEL_TILE = 256
N_SEL_BLOCKS_MAX = 64


def _params(sem):
    return pltpu.CompilerParams(dimension_semantics=sem, vmem_limit_bytes=VMEM_LIMIT)


def _mm_kernel(a_ref, b_ref, o_ref):
    o_ref[...] = jnp.dot(a_ref[...], b_ref[...], preferred_element_type=F32).astype(o_ref.dtype)


def _matmul(a, b, out_dtype, tm, tn, name):
    m, k = a.shape
    n = b.shape[1]
    assert m % tm == 0 and n % tn == 0
    return pl.pallas_call(
        _mm_kernel,
        grid=(m // tm, n // tn),
        in_specs=[pl.BlockSpec((tm, k), lambda i, j: (i, 0)),
                  pl.BlockSpec((k, tn), lambda i, j: (0, j))],
        out_specs=pl.BlockSpec((tm, tn), lambda i, j: (i, j)),
        out_shape=jax.ShapeDtypeStruct((m, n), out_dtype),
        compiler_params=_params(("parallel", "arbitrary")),
        name=name,
    )(a, b)


def _mm_acc_kernel(a_ref, b_ref, o_ref):
    k = pl.program_id(2)

    @pl.when(k == 0)
    def _():
        o_ref[...] = jnp.dot(a_ref[...], b_ref[...], preferred_element_type=F32)

    @pl.when(k > 0)
    def _():
        o_ref[...] += jnp.dot(a_ref[...], b_ref[...], preferred_element_type=F32)


def _matmul_kgrid(a, b, tm, tn, tk, name):
    m, k = a.shape
    n = b.shape[1]
    assert m % tm == 0 and n % tn == 0 and k % tk == 0
    return pl.pallas_call(
        _mm_acc_kernel,
        grid=(m // tm, n // tn, k // tk),
        in_specs=[pl.BlockSpec((tm, tk), lambda i, j, kk: (i, kk)),
                  pl.BlockSpec((tk, tn), lambda i, j, kk: (kk, j))],
        out_specs=pl.BlockSpec((tm, tn), lambda i, j, kk: (i, j)),
        out_shape=jax.ShapeDtypeStruct((m, n), F32),
        compiler_params=_params(("parallel", "arbitrary", "arbitrary")),
        name=name,
    )(a, b)


def _misc_kernel(x_ref, wh_ref, wl_ref, o_ref):
    x = x_ref[...]
    xh = x.astype(BF16)
    xl = (x - xh.astype(F32)).astype(BF16)
    wh = wh_ref[...]
    acc = jnp.dot(xh, wh, preferred_element_type=F32)
    acc += jnp.dot(xl, wh, preferred_element_type=F32)
    acc += jnp.dot(xh, wl_ref[...], preferred_element_type=F32)
    o_ref[...] = acc


def _misc_proj(x, w_hi, w_lo, tm):
    m, k = x.shape
    n = w_hi.shape[1]
    return pl.pallas_call(
        _misc_kernel,
        grid=(m // tm,),
        in_specs=[pl.BlockSpec((tm, k), lambda i: (i, 0)),
                  pl.BlockSpec((k, n), lambda i: (0, 0)),
                  pl.BlockSpec((k, n), lambda i: (0, 0))],
        out_specs=pl.BlockSpec((tm, n), lambda i: (i, 0)),
        out_shape=jax.ShapeDtypeStruct((m, n), F32),
        compiler_params=_params(("parallel",)),
        name="misc_proj",
    )(x, w_hi, w_lo)


def _swiglu_up_kernel(x_ref, wg_ref, wu_ref, o_ref, *, n_valid):
    x = x_ref[...]
    g = jnp.dot(x, wg_ref[...].astype(BF16), preferred_element_type=F32)
    u = jnp.dot(x, wu_ref[...].astype(BF16), preferred_element_type=F32)
    col = pl.program_id(1) * o_ref.shape[1] + lax.broadcasted_iota(jnp.int32, o_ref.shape, 1)
    o_ref[...] = jnp.where(col < n_valid, g * jax.nn.sigmoid(g) * u, 0.0).astype(o_ref.dtype)


def _swiglu_up(x, wg, wu, n_out, tm, tn):
    m, k = x.shape
    n = wg.shape[1]
    assert n_out % tn == 0 and n_out - n < tn
    return pl.pallas_call(
        functools.partial(_swiglu_up_kernel, n_valid=n),
        grid=(m // tm, n_out // tn),
        in_specs=[pl.BlockSpec((tm, k), lambda i, j: (i, 0), pipeline_mode=pl.Buffered(1)),
                  pl.BlockSpec((k, tn), lambda i, j: (0, j)),
                  pl.BlockSpec((k, tn), lambda i, j: (0, j))],
        out_specs=pl.BlockSpec((tm, tn), lambda i, j: (i, j)),
        out_shape=jax.ShapeDtypeStruct((m, n_out), BF16),
        compiler_params=_params(("parallel", "arbitrary")),
        name="swiglu_up",
    )(x, wg, wu)


def _add_ln_kernel(x_ref, y_ref, g_ref, b_ref, o_ref, ob_ref):
    z = DN_ALPHA * x_ref[...] + y_ref[...]
    mu = jnp.mean(z, axis=-1, keepdims=True)
    zc = z - mu
    var = jnp.mean(zc * zc, axis=-1, keepdims=True)
    out = zc * lax.rsqrt(var + LN_EPS) * g_ref[...] + b_ref[...]
    o_ref[...] = out
    ob_ref[...] = out.astype(BF16)


def _add_ln(x, y, g, b, tm=256):
    m, d = x.shape
    row = pl.BlockSpec((tm, d), lambda i: (i, 0))
    vec = pl.BlockSpec((1, d), lambda i: (0, 0))
    return pl.pallas_call(
        _add_ln_kernel,
        grid=(m // tm,),
        in_specs=[row, row, vec, vec],
        out_specs=[row, row],
        out_shape=[jax.ShapeDtypeStruct((m, d), F32), jax.ShapeDtypeStruct((m, d), BF16)],
        compiler_params=_params(("parallel",)),
        name="add_ln",
    )(x, y, g.reshape(1, d), b.reshape(1, d))


def _soft_cap(z):
    return A_GATE_CAP * jnp.tanh(z / A_GATE_CAP)


def _log_sigmoid(z):
    return jnp.minimum(z, 0.0) - jnp.log1p(jnp.exp(-jnp.abs(z)))


def _mlstm_kernel(q_ref, k_ref, v_ref, og_ref, gr_ref, gc_ref, bias_ref, norm_ref, o_ref,
                  c_ref, n_ref, m_ref):
    L = q_ref.shape[0]
    c = pl.program_id(2)

    @pl.when(c == 0)
    def _():
        c_ref[...] = jnp.zeros_like(c_ref)
        n_ref[...] = jnp.zeros_like(n_ref)
        m_ref[...] = jnp.zeros_like(m_ref)

    bias = bias_ref[...]
    gr = gr_ref[...]
    gc = gc_ref[...]
    li_r = _soft_cap(gr[0:1, :] + bias[:, 0:1])
    lf_r = _log_sigmoid(_soft_cap(gr[1:2, :] + bias[:, 1:2]))
    li_c = _soft_cap(gc[:, 0:1] + bias[:, 0:1])
    lf_c = _log_sigmoid(_soft_cap(gc[:, 1:2] + bias[:, 1:2]))

    t_idx = lax.broadcasted_iota(jnp.int32, (L, L), 0)
    s_idx = lax.broadcasted_iota(jnp.int32, (L, L), 1)
    causal = s_idx <= t_idx
    b_c = jnp.sum(jnp.where(causal, lf_r, 0.0), axis=1, keepdims=True)
    b_r = jnp.sum(jnp.where(t_idx <= s_idx, lf_c, 0.0), axis=0, keepdims=True)
    g_tot = jnp.sum(lf_r, axis=1, keepdims=True)

    m_prev = m_ref[...]
    dmat = jnp.where(causal, b_c - b_r + li_r, NEG_INF)
    m_inter = b_c + m_prev
    m_t = jnp.maximum(m_inter, jnp.max(dmat, axis=1, keepdims=True))
    w_inter = jnp.exp(m_inter - m_t)
    pmat = jnp.exp(dmat - m_t)

    q = q_ref[...]
    k = k_ref[...] * (A_DQK ** -0.5)
    v = v_ref[...]
    qk = lax.dot_general(q, k, (((1,), (1,)), ((), ())), preferred_element_type=F32)
    s = qk * pmat
    num = w_inter * jnp.dot(q, c_ref[...].astype(BF16), preferred_element_type=F32)
    num += jnp.dot(s.astype(BF16), v, preferred_element_type=F32)
    den = w_inter * jnp.sum(q.astype(F32) * n_ref[...], axis=1, keepdims=True)
    den += jnp.sum(s, axis=1, keepdims=True)
    h = num / jnp.maximum(jnp.abs(den), jnp.exp(-m_t))

    hn = h * lax.rsqrt(jnp.mean(h * h, axis=-1, keepdims=True) + RMS_EPS) * norm_ref[...]
    o_ref[...] = (hn * jax.nn.sigmoid(og_ref[...].astype(F32))).astype(o_ref.dtype)

    ws_c = g_tot - b_c + li_c
    ws_r = g_tot - b_r + li_r
    m_new = jnp.maximum(g_tot + m_prev, jnp.max(ws_r, axis=1, keepdims=True))
    decay = jnp.exp(g_tot + m_prev - m_new)
    kw = k.astype(F32) * jnp.exp(ws_c - m_new)
    c_ref[...] = decay * c_ref[...] + lax.dot_general(
        kw.astype(BF16), v, (((0,), (0,)), ((), ())), preferred_element_type=F32)
    n_ref[...] = decay * n_ref[...] + jnp.sum(kw, axis=0, keepdims=True)
    m_ref[...] = m_new


def _mlstm(p0, gates_r, gates_c, bias, norm, bsz, seq):
    L = MLSTM_CHUNK
    nc = seq // L
    qb, kb, vb, ogb = P0_Q // A_DQK, P0_K // A_DQK, P0_V // A_DV, P0_OG // A_DV
    return pl.pallas_call(
        _mlstm_kernel,
        grid=(bsz, A_HEADS, nc),
        in_specs=[
            pl.BlockSpec((None, L, A_DQK), lambda b, h, c: (b, c, qb + h)),
            pl.BlockSpec((None, L, A_DQK), lambda b, h, c: (b, c, kb + h)),
            pl.BlockSpec((None, L, A_DV), lambda b, h, c: (b, c, vb + h)),
            pl.BlockSpec((None, L, A_DV), lambda b, h, c: (b, c, ogb + h)),
            pl.BlockSpec((None, None, None, 2, L), lambda b, h, c: (b, h, c, 0, 0)),
            pl.BlockSpec((None, None, L, 2), lambda b, h, c: (b, h, c, 0)),
            pl.BlockSpec((None, 1, 2), lambda b, h, c: (h, 0, 0)),
            pl.BlockSpec((None, 1, A_DV), lambda b, h, c: (h, 0, 0)),
        ],
        out_specs=pl.BlockSpec((None, L, A_DV), lambda b, h, c: (b, c, h)),
        out_shape=jax.ShapeDtypeStruct((bsz, seq, A_V), BF16),
        scratch_shapes=[pltpu.VMEM((A_DQK, A_DV), F32), pltpu.VMEM((1, A_DQK), F32), pltpu.VMEM((1, 1), F32)],
        compiler_params=_params(("parallel", "parallel", "arbitrary")),
        name="mlstm",
    )(p0, p0, p0, p0, gates_r, gates_c, bias, norm)


def _rms(x, g):
    return x * lax.rsqrt(jnp.mean(x * x, axis=-1, keepdims=True) + RMS_EPS) * g


def _rope_tile(x, cos, sina, sinb):
    return x * cos + pltpu.roll(x, 96, 1) * sina + pltpu.roll(x, 32, 1) * sinb


def _mla_prep_kernel(cq_ref, ckv_ref, kr_ref, qg_ref, kvg_ref, wuq_ref, wuk_ref, wuv_ref,
                     cos_ref, sina_ref, sinb_ref, q_ref, k_ref, v_ref):
    scale = (B_NOPE + B_ROPE) ** -0.5 * LOG2E
    cos, sina, sinb = cos_ref[...], sina_ref[...], sinb_ref[...]
    cqn = _rms(cq_ref[...].astype(F32), qg_ref[...]).astype(BF16)
    qb = jnp.dot(cqn, wuq_ref[...], preferred_element_type=F32)
    ckvn = _rms(ckv_ref[...].astype(F32), kvg_ref[...]).astype(BF16)
    kup = jnp.dot(ckvn, wuk_ref[...], preferred_element_type=F32)
    v_ref[...] = jnp.dot(ckvn, wuv_ref[...], preferred_element_type=F32).astype(BF16)
    krope = _rope_tile(kr_ref[...], cos, sina, sinb).astype(BF16)
    for h in range(B_HEADS):
        c0 = 2 * LANES * h
        q_ref[:, c0:c0 + LANES] = (qb[:, c0:c0 + LANES] * scale).astype(BF16)
        q_ref[:, c0 + LANES:c0 + 2 * LANES] = (
            _rope_tile(qb[:, c0 + LANES:c0 + 2 * LANES], cos, sina, sinb) * scale).astype(BF16)
        k_ref[:, c0:c0 + LANES] = kup[:, LANES * h:LANES * (h + 1)].astype(BF16)
        k_ref[:, c0 + LANES:c0 + 2 * LANES] = krope


def _mla_prep(p0, misc, qg, kvg, wuq, wuk, wuv, cos, sina, sinb, seq, tm):
    m = p0.shape[0]
    nrope = seq // tm
    hq = B_HEADS * 2 * LANES
    hv = B_HEADS * B_DV
    full = lambda shape: pl.BlockSpec(shape, lambda i: (0, 0))
    tab = pl.BlockSpec((tm, LANES), lambda i: (i % nrope, 0))
    return pl.pallas_call(
        _mla_prep_kernel,
        grid=(m // tm,),
        in_specs=[
            pl.BlockSpec((tm, B_Q_LORA), lambda i: (i, P0_CQ // B_Q_LORA)),
            pl.BlockSpec((tm, B_KV_LORA), lambda i: (i, P0_CKV // B_KV_LORA)),
            pl.BlockSpec((tm, LANES), lambda i: (i, 0)),
            full((1, B_Q_LORA)), full((1, B_KV_LORA)),
            full((B_Q_LORA, hq)), full((B_KV_LORA, hv)), full((B_KV_LORA, hv)),
            tab, tab, tab,
        ],
        out_specs=[pl.BlockSpec((tm, hq), lambda i: (i, 0)),
                   pl.BlockSpec((tm, hq), lambda i: (i, 0)),
                   pl.BlockSpec((tm, hv), lambda i: (i, 0))],
        out_shape=[jax.ShapeDtypeStruct((m, hq), BF16), jax.ShapeDtypeStruct((m, hq), BF16),
                   jax.ShapeDtypeStruct((m, hv), BF16)],
        compiler_params=_params(("parallel",)),
        name="mla_prep",
    )(p0, p0, misc, qg, kvg, wuq, wuk, wuv, cos, sina, sinb)


def _lane_fold(x, op):
    out = x[:, :LANES]
    for c in range(1, x.shape[1] // LANES):
        out = op(out, x[:, c * LANES:(c + 1) * LANES])
    return out


def _attn_reset(mrun_ref, acc_ref):
    mrun_ref[...] = jnp.full_like(mrun_ref, NEG_INF)
    acc_ref[...] = jnp.zeros_like(acc_ref)


def _score_tile(j, s, s_ref, mrun_ref):
    s_ref[j] = s
    mrun_ref[...] = jnp.maximum(mrun_ref[...], _lane_fold(s, jnp.maximum))


def _rowmax_to_lanes(mrun_ref):
    m = jnp.max(mrun_ref[...], axis=1, keepdims=True)
    mrun_ref[...] = jnp.broadcast_to(m, mrun_ref.shape)


def _prob_tile(j, v, s_ref, mrun_ref, acc_ref):
    s = s_ref[j]
    mb = mrun_ref[...]
    p = jnp.concatenate([jnp.exp2(s[:, c * LANES:(c + 1) * LANES] - mb) for c in range(s.shape[1] // LANES)],
                        axis=1).astype(BF16)
    v_ones = jnp.concatenate([v, jnp.ones((v.shape[0], LANES), BF16)], axis=1)
    acc_ref[...] += jnp.dot(p, v_ones, preferred_element_type=F32)


def _attn_result(acc_ref, hd):
    acc = acc_ref[...]
    return acc[:, :hd] / acc[:, hd:], acc[:, hd:]


def _for_tiles(n, body):
    if isinstance(n, int):
        for j in range(n):
            body(j)
        return

    def pair(t, c):
        body(2 * t)
        body(2 * t + 1)
        return c

    lax.fori_loop(0, n // 2, pair, 0)

    @pl.when(n % 2 == 1)
    def _():
        body(n - 1)


_NT = (((1,), (1,)), ((), ()))


def _mla_attn_kernel(q_ref, k_ref, v_ref, o_ref, s_ref, mrun_ref, acc_ref):
    t = q_ref.shape[0]
    i = pl.program_id(2)
    q = q_ref[...]
    _attn_reset(mrun_ref, acc_ref)

    def scores(j):
        k0 = pl.multiple_of(j * t, t)
        return lax.dot_general(q, k_ref[pl.ds(k0, t), :], _NT, preferred_element_type=F32)

    _for_tiles(i, lambda j: _score_tile(j, scores(j), s_ref, mrun_ref))
    rows = lax.broadcasted_iota(jnp.int32, (t, t), 0)
    cols = lax.broadcasted_iota(jnp.int32, (t, t), 1)
    _score_tile(i, jnp.where(cols <= rows, scores(i), NEG_INF), s_ref, mrun_ref)
    _rowmax_to_lanes(mrun_ref)
    _for_tiles(i + 1, lambda j: _prob_tile(j, v_ref[pl.ds(pl.multiple_of(j * t, t), t), :],
                                           s_ref, mrun_ref, acc_ref))
    o_ref[...] = _attn_result(acc_ref, B_DV)[0].astype(o_ref.dtype)


def _mla_attn(qc, kc, vv, bsz, seq, t):
    n = seq // t
    return pl.pallas_call(
        _mla_attn_kernel,
        grid=(bsz, B_HEADS, n),
        in_specs=[pl.BlockSpec((None, t, 2 * LANES), lambda b, h, i: (b, i, h)),
                  pl.BlockSpec((None, seq, 2 * LANES), lambda b, h, i: (b, 0, h)),
                  pl.BlockSpec((None, seq, B_DV), lambda b, h, i: (b, 0, h))],
        out_specs=pl.BlockSpec((None, t, B_DV), lambda b, h, i: (b, i, h)),
        out_shape=jax.ShapeDtypeStruct((bsz, seq, B_HEADS * B_DV), BF16),
        scratch_shapes=[pltpu.VMEM((n, t, t), F32), pltpu.VMEM((t, LANES), F32),
                        pltpu.VMEM((t, B_DV + LANES), F32)],
        compiler_params=_params(("parallel", "parallel", "arbitrary")),
        name="mla_attn",
    )(qc, kc, vv)


def _gelu_tanh(x):
    return 0.5 * x * (1.0 + jnp.tanh(math.sqrt(2.0 / math.pi) * (x + 0.044715 * (x * x * x))))


def _compress_kernel(seg_ref, pe_ref, w1_ref, w2_ref, o_ref):
    half = seg_ref.shape[1]
    nseg = seg_ref.shape[0]
    seg = seg_ref[...].astype(F32)
    pe = pe_ref[...]
    top = (seg + pe[:, :half]).astype(BF16)
    bot = (seg + pe[:, half:]).astype(BF16)
    u = jnp.dot(top, w1_ref[:half, :], preferred_element_type=F32)
    w = jnp.dot(bot, w1_ref[half:, :], preferred_element_type=F32)
    hid = u + pltpu.roll(w, nseg - 1, 0)
    o_ref[...] = jnp.dot(_gelu_tanh(hid).astype(BF16), w2_ref[...], preferred_element_type=F32).astype(BF16)


def _compress(segs, pe, w1, w2):
    _, bsz, g, nseg, width = segs.shape
    return pl.pallas_call(
        _compress_kernel,
        grid=(2, bsz, g),
        in_specs=[pl.BlockSpec((None, None, None, nseg, width), lambda t, b, gg: (t, b, gg, 0, 0)),
                  pl.BlockSpec((None, 1, 2 * width), lambda t, b, gg: (t, 0, 0)),
                  pl.BlockSpec((None, 2 * width, C_CMP_HIDDEN), lambda t, b, gg: (t, 0, 0)),
                  pl.BlockSpec((None, C_CMP_HIDDEN, C_HD), lambda t, b, gg: (t, 0, 0))],
        out_specs=pl.BlockSpec((None, None, None, nseg, C_HD), lambda t, b, gg: (t, b, gg, 0, 0)),
        out_shape=jax.ShapeDtypeStruct((2, bsz, g, nseg, C_HD), BF16),
        compiler_params=_params(("parallel", "parallel", "parallel")),
        name="nsa_compress",
    )(segs, pe, w1, w2)


def _split2(x):
    hi = x.astype(BF16)
    return hi, (x - hi.astype(F32)).astype(BF16)


def _topk_blocks(imp_t2, i):
    nsel, width = imp_t2.shape
    half_w, half_n = width // 2, nsel // 2
    sblk = lax.broadcasted_iota(jnp.int32, imp_t2.shape, 0)
    upper = lax.broadcasted_iota(jnp.int32, imp_t2.shape, 1) >= half_w
    forced = (sblk == 0) | (sblk == i) | (sblk == i - 1)
    key = lax.bitcast_convert_type(jnp.where(forced, FORCE_SCORE, jnp.abs(imp_t2)), jnp.int32)
    key = jnp.where(sblk <= i, key, -(2 ** 30))
    key_m1 = key - 1
    sblk_adj = sblk - jnp.where(upper, half_n, 0)
    rank = jnp.zeros(imp_t2.shape, jnp.int32)
    for t in range(half_n):
        cand = jnp.where(upper[0:1], key[t + half_n:t + half_n + 1, :], key[t:t + 1, :])
        thr = jnp.where(sblk_adj > t, key_m1, key)
        rank = rank + jnp.where(cand > thr, 1, 0)
    rank = rank + pltpu.roll(rank, half_w, 1)
    return rank < C_N_SEL


def _nsa_branch(q, k_ref, v_ref, n_tiles, k0_fn, bias_ref, s_ref, mrun_ref, acc_ref):
    R, QB = q.shape[0], bias_ref.shape[1]
    _attn_reset(mrun_ref, acc_ref)

    def p1(j):
        s = lax.dot_general(q, k_ref[pl.ds(k0_fn(j), SEL_TILE), :], _NT, preferred_element_type=F32)
        s = (s.reshape(R // QB, QB, SEL_TILE) + bias_ref[j][None]).reshape(R, SEL_TILE)
        _score_tile(j, s, s_ref, mrun_ref)

    _for_tiles(n_tiles, p1)
    _rowmax_to_lanes(mrun_ref)
    _for_tiles(n_tiles, lambda j: _prob_tile(j, v_ref[pl.ds(k0_fn(j), SEL_TILE), :], s_ref, mrun_ref, acc_ref))
    return _attn_result(acc_ref, C_HD)[0]


def _nsa_kernel(q_ref, gate_ref, bg_ref, kcmp_ref, vcmp_ref, ks_ref, vs_ref, kw_ref, vw_ref,
                cext_ref, exp_ref, gsel_ref, o_ref, s_ref, bias_ref, wbias_ref, gexp_ref, mrun_ref, acc_ref,
                out_ref):
    QB, HG, HD = C_Q_BLOCK, C_HG, C_HD
    R = QB * HG
    i = pl.program_id(2)
    t0 = i * QB
    scale = HD ** -0.5 * LOG2E

    q_all = q_ref[...]
    q = jnp.concatenate([q_all[:, h * HD:(h + 1) * HD] for h in range(HG)], axis=0)
    q = (q.astype(F32) * scale).astype(BF16)
    tq = t0 + lax.broadcasted_iota(jnp.int32, (QB, 1), 0)
    lane = lax.broadcasted_iota(jnp.int32, (QB, SEL_TILE), 1)

    g_hi, g_lo = _split2(jax.nn.sigmoid(gate_ref[...].astype(F32) + bg_ref[...]))
    gexp_ref[...] = (jnp.dot(g_hi, gsel_ref[...], preferred_element_type=F32)
                     + jnp.dot(g_lo, gsel_ref[...], preferred_element_type=F32))

    def gate_rows(br):
        return jnp.concatenate([gexp_ref[:, (br * HG + h) * LANES:(br * HG + h + 1) * LANES] for h in range(HG)],
                               axis=0)

    ncmp = kcmp_ref.shape[0]
    vis = lax.broadcasted_iota(jnp.int32, (QB, ncmp), 1) * C_CMP_STRIDE + (C_CMP_LEN - 1) <= tq
    s_c = lax.dot_general(q, kcmp_ref[...], _NT, preferred_element_type=F32).reshape(HG, QB, ncmp)
    s_c = s_c + jnp.where(vis, 0.0, NEG_INF)[None]
    e = jnp.exp2(s_c - jnp.max(s_c, axis=2, keepdims=True)) * jnp.where(vis, 1.0, 0.0)[None]
    e_hi, e_lo = _split2(e.reshape(R, ncmp))
    cext = cext_ref[...]
    r_hi = jnp.dot(e_hi, jnp.concatenate([vcmp_ref[...], cext], axis=1), preferred_element_type=F32)
    r_lo = jnp.dot(e_lo, cext, preferred_element_type=F32)
    l_c = r_hi[:, HD:HD + LANES] + r_lo[:, :LANES]
    inv = 1.0 / jnp.where(l_c > 0.0, l_c, 1.0)
    out_ref[...] = gate_rows(0) * (r_hi[:, :HD] * inv)

    imp = jnp.sum(((r_hi[:, HD + LANES:] + r_lo[:, LANES:]) * inv).reshape(HG, QB, LANES), axis=0)
    imp_t = jnp.concatenate([imp, jnp.zeros((LANES - QB, LANES), F32)], axis=0).T
    nsel = exp_ref.shape[0]
    imp_t = imp_t[:nsel]
    sel = _topk_blocks(imp_t + pltpu.roll(imp_t, QB, 1), i)
    drop = jnp.where(sel[:, :QB], 0.0, -1.0).astype(BF16)
    bias = lax.dot_general(drop, exp_ref[...], (((0,), (0,)), ((), ())), preferred_element_type=F32)
    for j in range(bias_ref.shape[0]):
        bias_ref[j] = bias[:, j * SEL_TILE:(j + 1) * SEL_TILE]
    jd = (t0 + QB - 1) // SEL_TILE
    bias_ref[jd] = jnp.where(jd * SEL_TILE + lane <= tq, bias_ref[jd], NEG_INF)

    o_s = _nsa_branch(q, ks_ref, vs_ref, jd + 1, lambda j: pl.multiple_of(j * SEL_TILE, SEL_TILE),
                      bias_ref, s_ref, mrun_ref, acc_ref)
    out_ref[...] += gate_rows(1) * o_s

    n_win = wbias_ref.shape[0]
    kbase = jnp.maximum(t0 + QB - n_win * SEL_TILE, 0)
    for w in range(n_win):
        kpos = kbase + w * SEL_TILE + lane
        wbias_ref[w] = jnp.where((kpos <= tq) & (kpos > tq - C_WINDOW), 0.0, NEG_INF)
    o_w = _nsa_branch(q, kw_ref, vw_ref, n_win, lambda j: pl.multiple_of(kbase + j * SEL_TILE, QB),
                      wbias_ref, s_ref, mrun_ref, acc_ref)
    out = out_ref[...] + gate_rows(2) * o_w
    o_ref[...] = jnp.concatenate([out[h * QB:(h + 1) * QB] for h in range(HG)], axis=1).astype(o_ref.dtype)


def _nsa_attn(p1, cmp_kv, bg, bsz, seq):
    QB, HD, G = C_Q_BLOCK, C_HD, C_KV_GROUPS
    nq = seq // QB
    ncmp = seq // C_CMP_STRIDE
    nsel = seq // C_SEL_LEN
    n_tiles = seq // SEL_TILE
    n_win = (C_WINDOW + QB + SEL_TILE - 1) // SEL_TILE
    R = QB * C_HG
    qw = C_HG * HD
    n_gate = 3 * C_HG
    assert nsel <= LANES and n_gate <= LANES

    cmp_start = np.arange(ncmp) * C_CMP_STRIDE
    sel_start = np.arange(nsel) * C_SEL_LEN
    ovl = ((cmp_start[:, None] < sel_start[None, :] + C_SEL_LEN)
           & (cmp_start[:, None] + C_CMP_LEN > sel_start[None, :])).astype(np.float32)
    cext = np.concatenate([np.ones((ncmp, LANES), np.float32), ovl, np.zeros((ncmp, LANES - nsel), np.float32)], 1)
    expand = (np.arange(seq)[None, :] // C_SEL_LEN == np.arange(nsel)[:, None]).astype(np.float32) * -NEG_INF
    gsel = (np.arange(n_gate * LANES)[None, :] // LANES == np.arange(LANES)[:, None]).astype(np.float32)
    cext, expand, gsel = (jnp.asarray(a, BF16) for a in (cext, expand, gsel))

    def kv_spec(col0):
        return pl.BlockSpec((None, seq, HD), lambda b, g, i: (b, 0, col0 // HD + g))

    return pl.pallas_call(
        _nsa_kernel,
        grid=(bsz, G, nq),
        in_specs=[
            pl.BlockSpec((None, QB, qw), lambda b, g, i: (b, i, g)),
            pl.BlockSpec((None, QB, LANES), lambda b, g, i: (b, i, P1_G // LANES + g)),
            pl.BlockSpec((None, 1, LANES), lambda b, g, i: (g, 0, 0)),
            pl.BlockSpec((None, None, None, ncmp, HD), lambda b, g, i: (0, b, g, 0, 0)),
            pl.BlockSpec((None, None, None, ncmp, HD), lambda b, g, i: (1, b, g, 0, 0)),
            kv_spec(P1_KS), kv_spec(P1_VS), kv_spec(P1_KW), kv_spec(P1_VW),
            pl.BlockSpec((ncmp, 2 * LANES), lambda b, g, i: (0, 0)),
            pl.BlockSpec((nsel, seq), lambda b, g, i: (0, 0)),
            pl.BlockSpec((LANES, n_gate * LANES), lambda b, g, i: (0, 0)),
        ],
        out_specs=pl.BlockSpec((None, QB, qw), lambda b, g, i: (b, i, g)),
        out_shape=jax.ShapeDtypeStruct((bsz, seq, C_HEADS * HD), BF16),
        scratch_shapes=[pltpu.VMEM((n_tiles, R, SEL_TILE), F32),
                        pltpu.VMEM((n_tiles, QB, SEL_TILE), F32),
                        pltpu.VMEM((n_win, QB, SEL_TILE), F32),
                        pltpu.VMEM((QB, n_gate * LANES), F32),
                        pltpu.VMEM((R, LANES), F32),
                        pltpu.VMEM((R, HD + LANES), F32),
                        pltpu.VMEM((R, HD), F32)],
        compiler_params=_params(("parallel", "parallel", "arbitrary")),
        name="nsa_attn",
    )(p1, p1, bg, cmp_kv, cmp_kv, p1, p1, p1, p1, cext, expand, gsel)


def _split_hi_lo(w):
    hi = w.astype(BF16)
    return hi, (w - hi.astype(F32)).astype(BF16)


def _rope_tables(seq):
    half = B_ROPE // 2
    inv = ROPE_THETA ** (-jnp.arange(half, dtype=F32) / half)
    ang = jnp.arange(seq, dtype=F32)[:, None] * inv[None, :]
    cos, sin = jnp.cos(ang), jnp.sin(ang)
    z = jnp.zeros((seq, LANES - B_ROPE), F32)
    zh = jnp.zeros((seq, half), F32)
    return (jnp.concatenate([cos, cos, z], 1), jnp.concatenate([-sin, zh, z], 1),
            jnp.concatenate([zh, sin, z], 1))


def _ffn(x_bf, wg, wu, wd):
    wd = jnp.pad(wd, ((0, FFN_PAD - FFN_HIDDEN), (0, 0))).astype(BF16)
    h = _swiglu_up(x_bf, wg, wu, FFN_PAD, tm=1024, tn=512)
    return _matmul_kgrid(h, wd, tm=1024, tn=2048, tk=1024, name="ffn_down")


def _mixer_ab(x2, x2_bf, bsz, seq, w_in, b_ig, b_fg, mlstm_norm, q_norm, kv_norm, w_uq, w_ukv, w_o):
    m = bsz * seq
    wq, wk, wv, wig, wfg, wog, wcq, wckv, wkr = jnp.split(w_in, AB_SPLITS, axis=1)
    w_main = jnp.concatenate([wq, wk, wv, wog, wcq, wckv], 1).astype(BF16)
    w_misc = jnp.concatenate([wkr, wig, wfg, jnp.zeros((D_MODEL, LANES - B_ROPE - 2 * A_HEADS), F32)], 1)
    p0 = _matmul(x2_bf, w_main, BF16, tm=1024, tn=768, name="proj_ab")
    misc = _misc_proj(x2, *_split_hi_lo(w_misc), tm=512)

    L = MLSTM_CHUNK
    gates = misc[:, MISC_IG:MISC_IG + 2 * A_HEADS].reshape(bsz, seq, 2, A_HEADS)
    gates_c = jnp.transpose(gates, (0, 3, 1, 2))
    gates_r = jnp.transpose(gates.reshape(bsz, seq // L, L, 2, A_HEADS), (0, 4, 1, 3, 2))
    bias = jnp.stack([b_ig, b_fg], -1).reshape(A_HEADS, 1, 2)
    h_a = _mlstm(p0.reshape(bsz, seq, P0_N), gates_r, gates_c, bias,
                 mlstm_norm.reshape(A_HEADS, 1, A_DV), bsz, seq)

    wuq = jnp.pad(w_uq.reshape(B_Q_LORA, B_HEADS, B_NOPE + B_ROPE),
                  ((0, 0), (0, 0), (0, 2 * LANES - B_NOPE - B_ROPE))).reshape(B_Q_LORA, -1).astype(BF16)
    wukv = w_ukv.reshape(B_KV_LORA, B_HEADS, B_NOPE + B_DV)
    wuk = wukv[:, :, :B_NOPE].reshape(B_KV_LORA, -1).astype(BF16)
    wuv = wukv[:, :, B_NOPE:].reshape(B_KV_LORA, -1).astype(BF16)
    cos, sina, sinb = _rope_tables(seq)
    qc, kc, vv = _mla_prep(p0, misc, q_norm.reshape(1, -1), kv_norm.reshape(1, -1), wuq, wuk, wuv,
                           cos, sina, sinb, seq, tm=512)
    h_b = _mla_attn(qc.reshape(bsz, seq, -1), kc.reshape(bsz, seq, -1), vv.reshape(bsz, seq, -1),
                    bsz, seq, t=512)
    h = jnp.concatenate([h_a, h_b], -1).reshape(m, D_MODEL)
    return _matmul(h, w_o.astype(BF16), F32, tm=1024, tn=1024, name="out_ab")


def _mixer_c(x2_bf, bsz, seq, w_in, b_gate, pe_k, pe_v, w1_k, w2_k, w1_v, w2_v, w_o):
    m = bsz * seq
    G, HG, HD = C_KV_GROUPS, C_HG, C_HD
    wg = w_in[:, C_SPLITS[-1]:].reshape(D_MODEL, 3, G, HG)
    wg = jnp.transpose(wg, (0, 2, 1, 3)).reshape(D_MODEL, G, 3 * HG)
    wg = jnp.pad(wg, ((0, 0), (0, 0), (0, LANES - 3 * HG))).reshape(D_MODEL, G * LANES)
    w_main = jnp.concatenate([w_in[:, :C_SPLITS[-1]], wg], 1).astype(BF16)
    bg = jnp.transpose(b_gate.reshape(3, G, HG), (1, 0, 2)).reshape(G, 1, 3 * HG)
    bg = jnp.pad(bg, ((0, 0), (0, 0), (0, LANES - 3 * HG)))
    p1 = _matmul(x2_bf, w_main, BF16, tm=1024, tn=768, name="proj_c")
    p1 = p1.reshape(bsz, seq, P1_N)

    nseg = seq // C_CMP_STRIDE
    kv = p1[:, :, P1_KC:P1_KS].reshape(bsz, nseg, C_CMP_STRIDE, 2, G, HD)
    segs = jnp.transpose(kv, (3, 0, 4, 1, 2, 5)).reshape(2, bsz, G, nseg, C_CMP_STRIDE * HD)
    pe = jnp.stack([pe_k.reshape(1, -1), pe_v.reshape(1, -1)])
    w1 = jnp.stack([w1_k, w1_v]).astype(BF16)
    w2 = jnp.stack([w2_k, w2_v]).astype(BF16)
    cmp_kv = _compress(segs, pe, w1, w2)

    out = _nsa_attn(p1, cmp_kv, bg, bsz, seq)
    return _matmul(out.reshape(m, D_MODEL), w_o.astype(BF16), F32, tm=1024, tn=1024, name="out_c")


def kernel(x, ab_w_in, ab_b_igate, ab_b_fgate, ab_mlstm_norm, ab_q_norm, ab_kv_norm, ab_w_uq, ab_w_ukv, ab_w_o, c_w_in, c_b_gate, c_pe_k, c_pe_v, c_cmp_w1_k, c_cmp_w2_k, c_cmp_w1_v, c_cmp_w2_v, c_w_o, ffn_w_gate, ffn_w_up, ffn_w_down, ln_mix_g, ln_mix_b, ln_ffn_g, ln_ffn_b):
    bsz, seq, d = x.shape
    m = bsz * seq
    x2 = x.reshape(m, d)
    x2_bf = x2.astype(BF16)
    for layer in range(DEPTH):
        j = layer // 2
        if layer % 2 == 0:
            y = _mixer_ab(x2, x2_bf, bsz, seq, ab_w_in[j], ab_b_igate[j], ab_b_fgate[j], ab_mlstm_norm[j],
                          ab_q_norm[j], ab_kv_norm[j], ab_w_uq[j], ab_w_ukv[j], ab_w_o[j])
        else:
            y = _mixer_c(x2_bf, bsz, seq, c_w_in[j], c_b_gate[j], c_pe_k[j], c_pe_v[j], c_cmp_w1_k[j],
                         c_cmp_w2_k[j], c_cmp_w1_v[j], c_cmp_w2_v[j], c_w_o[j])
        x2, x2_bf = _add_ln(x2, y, ln_mix_g[layer], ln_mix_b[layer])
        y = _ffn(x2_bf, ffn_w_gate[layer], ffn_w_up[layer], ffn_w_down[layer])
        x2, x2_bf = _add_ln(x2, y, ln_ffn_g[layer], ln_ffn_b[layer])
    return x2.reshape(bsz, seq, d)
```

```python
import functools
import math

import numpy as np
import jax
import jax.numpy as jnp
from jax import lax
from jax.experimental import pallas as pl
from jax.experimental.pallas import tpu as pltpu

F32 = jnp.float32
BF16 = jnp.bfloat16

D_MODEL = 4096
DEPTH = 2
DN_ALPHA = (2 * DEPTH) ** 0.25
LN_EPS = 1e-5
RMS_EPS = 1e-6
NEG_INF = -1e30
FORCE_SCORE = 1e9
LOG2E = 1.4426950408889634

A_HEADS, A_DQK, A_DV = 4, 256, 512
A_QK, A_V = A_HEADS * A_DQK, A_HEADS * A_DV
A_GATE_CAP = 15.0
B_HEADS, B_Q_LORA, B_KV_LORA, B_NOPE, B_ROPE, B_DV = 16, 1024, 512, 128, 64, 128
ROPE_THETA = 10000.0
C_HEADS, C_KV_GROUPS, C_HD = 32, 4, 128
C_HG = C_HEADS // C_KV_GROUPS
C_KV = C_KV_GROUPS * C_HD
C_CMP_LEN, C_CMP_STRIDE, C_SEL_LEN, C_N_SEL, C_WINDOW, C_CMP_HIDDEN = 32, 16, 64, 16, 512, 256
C_Q_BLOCK = 64
FFN_HIDDEN = 11008
FFN_PAD = 11264

AB_SPLITS = [int(v) for v in np.cumsum([A_QK, A_QK, A_V, A_HEADS, A_HEADS, A_V, B_Q_LORA, B_KV_LORA])]
C_SPLITS = [int(v) for v in np.cumsum([C_HEADS * C_HD] + [C_KV] * 6)]

LANES = 128
VMEM_LIMIT = 56 * 1024 * 1024

P0_Q, P0_K, P0_V, P0_OG, P0_CQ, P0_CKV, P0_N = 0, 1024, 2048, 4096, 6144, 7168, 7680
MISC_IG, MISC_FG = 64, 68
P1_Q, P1_KC, P1_VC, P1_KS, P1_VS, P1_KW, P1_VW, P1_G, P1_N = 0, 4096, 4608, 5120, 5632, 6144, 6656, 7168, 7680

MLSTM_CHUNK = 256
SEL_TILE = 256
N_SEL_BLOCKS_MAX = 64


def _params(sem):
    return pltpu.CompilerParams(dimension_semantics=sem, vmem_limit_bytes=VMEM_LIMIT)


def _mm_kernel(a_ref, b_ref, o_ref):
    o_ref[...] = jnp.dot(a_ref[...], b_ref[...], preferred_element_type=F32).astype(o_ref.dtype)


def _matmul(a, b, out_dtype, tm, tn, name):
    m, k = a.shape
    n = b.shape[1]
    assert m % tm == 0 and n % tn == 0
    return pl.pallas_call(
        _mm_kernel,
        grid=(m // tm, n // tn),
        in_specs=[pl.BlockSpec((tm, k), lambda i, j: (i, 0)),
                  pl.BlockSpec((k, tn), lambda i, j: (0, j))],
        out_specs=pl.BlockSpec((tm, tn), lambda i, j: (i, j)),
        out_shape=jax.ShapeDtypeStruct((m, n), out_dtype),
        compiler_params=_params(("parallel", "arbitrary")),
        name=name,
    )(a, b)


def _mm_acc_kernel(a_ref, b_ref, o_ref):
    k = pl.program_id(2)

    @pl.when(k == 0)
    def _():
        o_ref[...] = jnp.dot(a_ref[...], b_ref[...], preferred_element_type=F32)

    @pl.when(k > 0)
    def _():
        o_ref[...] += jnp.dot(a_ref[...], b_ref[...], preferred_element_type=F32)


def _matmul_kgrid(a, b, tm, tn, tk, name):
    m, k = a.shape
    n = b.shape[1]
    assert m % tm == 0 and n % tn == 0 and k % tk == 0
    return pl.pallas_call(
        _mm_acc_kernel,
        grid=(m // tm, n // tn, k // tk),
        in_specs=[pl.BlockSpec((tm, tk), lambda i, j, kk: (i, kk)),
                  pl.BlockSpec((tk, tn), lambda i, j, kk: (kk, j))],
        out_specs=pl.BlockSpec((tm, tn), lambda i, j, kk: (i, j)),
        out_shape=jax.ShapeDtypeStruct((m, n), F32),
        compiler_params=_params(("parallel", "arbitrary", "arbitrary")),
        name=name,
    )(a, b)


def _misc_kernel(x_ref, wh_ref, wl_ref, o_ref):
    x = x_ref[...]
    xh = x.astype(BF16)
    xl = (x - xh.astype(F32)).astype(BF16)
    wh = wh_ref[...]
    acc = jnp.dot(xh, wh, preferred_element_type=F32)
    acc += jnp.dot(xl, wh, preferred_element_type=F32)
    acc += jnp.dot(xh, wl_ref[...], preferred_element_type=F32)
    o_ref[...] = acc


def _misc_proj(x, w_hi, w_lo, tm):
    m, k = x.shape
    n = w_hi.shape[1]
    return pl.pallas_call(
        _misc_kernel,
        grid=(m // tm,),
        in_specs=[pl.BlockSpec((tm, k), lambda i: (i, 0)),
                  pl.BlockSpec((k, n), lambda i: (0, 0)),
                  pl.BlockSpec((k, n), lambda i: (0, 0))],
        out_specs=pl.BlockSpec((tm, n), lambda i: (i, 0)),
        out_shape=jax.ShapeDtypeStruct((m, n), F32),
        compiler_params=_params(("parallel",)),
        name="misc_proj",
    )(x, w_hi, w_lo)


def _swiglu_up_kernel(x_ref, wg_ref, wu_ref, o_ref, *, n_valid):
    x = x_ref[...]
    g = jnp.dot(x, wg_ref[...].astype(BF16), preferred_element_type=F32)
    u = jnp.dot(x, wu_ref[...].astype(BF16), preferred_element_type=F32)
    col = pl.program_id(1) * o_ref.shape[1] + lax.broadcasted_iota(jnp.int32, o_ref.shape, 1)
    o_ref[...] = jnp.where(col < n_valid, g * jax.nn.sigmoid(g) * u, 0.0).astype(o_ref.dtype)


def _swiglu_up(x, wg, wu, layer, n_out, tm, tn):
    m, k = x.shape
    n = wg.shape[2]
    assert n_out % tn == 0 and n_out - n < tn
    return pl.pallas_call(
        functools.partial(_swiglu_up_kernel, n_valid=n),
        grid=(m // tm, n_out // tn),
        in_specs=[pl.BlockSpec((tm, k), lambda i, j: (i, 0), pipeline_mode=pl.Buffered(1)),
                  pl.BlockSpec((None, k, tn), lambda i, j: (layer, 0, j)),
                  pl.BlockSpec((None, k, tn), lambda i, j: (layer, 0, j))],
        out_specs=pl.BlockSpec((tm, tn), lambda i, j: (i, j)),
        out_shape=jax.ShapeDtypeStruct((m, n_out), BF16),
        compiler_params=_params(("parallel", "arbitrary")),
        name="swiglu_up",
    )(x, wg, wu)


def _add_ln_kernel(x_ref, y_ref, g_ref, b_ref, o_ref, ob_ref):
    z = DN_ALPHA * x_ref[...] + y_ref[...]
    mu = jnp.mean(z, axis=-1, keepdims=True)
    zc = z - mu
    var = jnp.mean(zc * zc, axis=-1, keepdims=True)
    out = zc * lax.rsqrt(var + LN_EPS) * g_ref[...] + b_ref[...]
    o_ref[...] = out
    ob_ref[...] = out.astype(BF16)


def _add_ln(x, y, g, b, tm=256):
    m, d = x.shape
    row = pl.BlockSpec((tm, d), lambda i: (i, 0))
    vec = pl.BlockSpec((1, d), lambda i: (0, 0))
    return pl.pallas_call(
        _add_ln_kernel,
        grid=(m // tm,),
        in_specs=[row, row, vec, vec],
        out_specs=[row, row],
        out_shape=[jax.ShapeDtypeStruct((m, d), F32), jax.ShapeDtypeStruct((m, d), BF16)],
        compiler_params=_params(("parallel",)),
        name="add_ln",
    )(x, y, g.reshape(1, d), b.reshape(1, d))


def _soft_cap(z):
    return A_GATE_CAP * jnp.tanh(z / A_GATE_CAP)


def _log_sigmoid(z):
    return jnp.minimum(z, 0.0) - jnp.log1p(jnp.exp(-jnp.abs(z)))


def _mlstm_kernel(q_ref, k_ref, v_ref, og_ref, gr_ref, gc_ref, bias_ref, norm_ref, o_ref,
                  c_ref, n_ref, m_ref):
    L = q_ref.shape[0]
    c = pl.program_id(2)

    @pl.when(c == 0)
    def _():
        c_ref[...] = jnp.zeros_like(c_ref)
        n_ref[...] = jnp.zeros_like(n_ref)
        m_ref[...] = jnp.zeros_like(m_ref)

    bias = bias_ref[...]
    gr = gr_ref[...]
    gc = gc_ref[...]
    li_r = _soft_cap(gr[0:1, :] + bias[:, 0:1])
    lf_r = _log_sigmoid(_soft_cap(gr[1:2, :] + bias[:, 1:2]))
    li_c = _soft_cap(gc[:, 0:1] + bias[:, 0:1])
    lf_c = _log_sigmoid(_soft_cap(gc[:, 1:2] + bias[:, 1:2]))

    t_idx = lax.broadcasted_iota(jnp.int32, (L, L), 0)
    s_idx = lax.broadcasted_iota(jnp.int32, (L, L), 1)
    causal = s_idx <= t_idx
    b_c = jnp.sum(jnp.where(causal, lf_r, 0.0), axis=1, keepdims=True)
    b_r = jnp.sum(jnp.where(t_idx <= s_idx, lf_c, 0.0), axis=0, keepdims=True)
    g_tot = jnp.sum(lf_r, axis=1, keepdims=True)

    m_prev = m_ref[...]
    dmat = jnp.where(causal, b_c - b_r + li_r, NEG_INF)
    m_inter = b_c + m_prev
    m_t = jnp.maximum(m_inter, jnp.max(dmat, axis=1, keepdims=True))
    w_inter = jnp.exp(m_inter - m_t)
    pmat = jnp.exp(dmat - m_t)

    q = q_ref[...]
    k = k_ref[...] * (A_DQK ** -0.5)
    v = v_ref[...]
    qk = lax.dot_general(q, k, (((1,), (1,)), ((), ())), preferred_element_type=F32)
    s = qk * pmat
    num = w_inter * jnp.dot(q, c_ref[...].astype(BF16), preferred_element_type=F32)
    num += jnp.dot(s.astype(BF16), v, preferred_element_type=F32)
    den = w_inter * jnp.sum(q.astype(F32) * n_ref[...], axis=1, keepdims=True)
    den += jnp.sum(s, axis=1, keepdims=True)
    h = num / jnp.maximum(jnp.abs(den), jnp.exp(-m_t))

    hn = h * lax.rsqrt(jnp.mean(h * h, axis=-1, keepdims=True) + RMS_EPS) * norm_ref[...]
    o_ref[...] = (hn * jax.nn.sigmoid(og_ref[...].astype(F32))).astype(o_ref.dtype)

    ws_c = g_tot - b_c + li_c
    ws_r = g_tot - b_r + li_r
    m_new = jnp.maximum(g_tot + m_prev, jnp.max(ws_r, axis=1, keepdims=True))
    decay = jnp.exp(g_tot + m_prev - m_new)
    kw = k.astype(F32) * jnp.exp(ws_c - m_new)
    c_ref[...] = decay * c_ref[...] + lax.dot_general(
        kw.astype(BF16), v, (((0,), (0,)), ((), ())), preferred_element_type=F32)
    n_ref[...] = decay * n_ref[...] + jnp.sum(kw, axis=0, keepdims=True)
    m_ref[...] = m_new


def _mlstm(p0, gates_r, gates_c, bias, norm, bsz, seq):
    L = MLSTM_CHUNK
    nc = seq // L
    qb, kb, vb, ogb = P0_Q // A_DQK, P0_K // A_DQK, P0_V // A_DV, P0_OG // A_DV
    return pl.pallas_call(
        _mlstm_kernel,
        grid=(bsz, A_HEADS, nc),
        in_specs=[
            pl.BlockSpec((None, L, A_DQK), lambda b, h, c: (b, c, qb + h)),
            pl.BlockSpec((None, L, A_DQK), lambda b, h, c: (b, c, kb + h)),
            pl.BlockSpec((None, L, A_DV), lambda b, h, c: (b, c, vb + h)),
            pl.BlockSpec((None, L, A_DV), lambda b, h, c: (b, c, ogb + h)),
            pl.BlockSpec((None, None, None, 2, L), lambda b, h, c: (b, h, c, 0, 0)),
            pl.BlockSpec((None, None, L, 2), lambda b, h, c: (b, h, c, 0)),
            pl.BlockSpec((None, 1, 2), lambda b, h, c: (h, 0, 0)),
            pl.BlockSpec((None, 1, A_DV), lambda b, h, c: (h, 0, 0)),
        ],
        out_specs=pl.BlockSpec((None, L, A_DV), lambda b, h, c: (b, c, h)),
        out_shape=jax.ShapeDtypeStruct((bsz, seq, A_V), BF16),
        scratch_shapes=[pltpu.VMEM((A_DQK, A_DV), F32), pltpu.VMEM((1, A_DQK), F32), pltpu.VMEM((1, 1), F32)],
        compiler_params=_params(("parallel", "parallel", "arbitrary")),
        name="mlstm",
    )(p0, p0, p0, p0, gates_r, gates_c, bias, norm)


def _rms(x, g):
    return x * lax.rsqrt(jnp.mean(x * x, axis=-1, keepdims=True) + RMS_EPS) * g


def _rope_tile(x, cos, sina, sinb):
    return x * cos + pltpu.roll(x, 96, 1) * sina + pltpu.roll(x, 32, 1) * sinb


def _mla_prep_kernel(cq_ref, ckv_ref, kr_ref, qg_ref, kvg_ref, wuq_ref, wuk_ref, wuv_ref,
                     cos_ref, sina_ref, sinb_ref, q_ref, k_ref, v_ref):
    scale = (B_NOPE + B_ROPE) ** -0.5 * LOG2E
    cos, sina, sinb = cos_ref[...], sina_ref[...], sinb_ref[...]
    cqn = _rms(cq_ref[...].astype(F32), qg_ref[...]).astype(BF16)
    qb = jnp.dot(cqn, wuq_ref[...], preferred_element_type=F32)
    ckvn = _rms(ckv_ref[...].astype(F32), kvg_ref[...]).astype(BF16)
    kup = jnp.dot(ckvn, wuk_ref[...], preferred_element_type=F32)
    v_ref[...] = jnp.dot(ckvn, wuv_ref[...], preferred_element_type=F32).astype(BF16)
    krope = _rope_tile(kr_ref[...], cos, sina, sinb).astype(BF16)
    for h in range(B_HEADS):
        c0 = 2 * LANES * h
        q_ref[:, c0:c0 + LANES] = (qb[:, c0:c0 + LANES] * scale).astype(BF16)
        q_ref[:, c0 + LANES:c0 + 2 * LANES] = (
            _rope_tile(qb[:, c0 + LANES:c0 + 2 * LANES], cos, sina, sinb) * scale).astype(BF16)
        k_ref[:, c0:c0 + LANES] = kup[:, LANES * h:LANES * (h + 1)].astype(BF16)
        k_ref[:, c0 + LANES:c0 + 2 * LANES] = krope


def _mla_prep(p0, misc, qg, kvg, wuq, wuk, wuv, cos, sina, sinb, seq, tm):
    m = p0.shape[0]
    nrope = seq // tm
    hq = B_HEADS * 2 * LANES
    hv = B_HEADS * B_DV
    full = lambda shape: pl.BlockSpec(shape, lambda i: (0, 0))
    tab = pl.BlockSpec((tm, LANES), lambda i: (i % nrope, 0))
    return pl.pallas_call(
        _mla_prep_kernel,
        grid=(m // tm,),
        in_specs=[
            pl.BlockSpec((tm, B_Q_LORA), lambda i: (i, P0_CQ // B_Q_LORA)),
            pl.BlockSpec((tm, B_KV_LORA), lambda i: (i, P0_CKV // B_KV_LORA)),
            pl.BlockSpec((tm, LANES), lambda i: (i, 0)),
            full((1, B_Q_LORA)), full((1, B_KV_LORA)),
            full((B_Q_LORA, hq)), full((B_KV_LORA, hv)), full((B_KV_LORA, hv)),
            tab, tab, tab,
        ],
        out_specs=[pl.BlockSpec((tm, hq), lambda i: (i, 0)),
                   pl.BlockSpec((tm, hq), lambda i: (i, 0)),
                   pl.BlockSpec((tm, hv), lambda i: (i, 0))],
        out_shape=[jax.ShapeDtypeStruct((m, hq), BF16), jax.ShapeDtypeStruct((m, hq), BF16),
                   jax.ShapeDtypeStruct((m, hv), BF16)],
        compiler_params=_params(("parallel",)),
        name="mla_prep",
    )(p0, p0, misc, qg, kvg, wuq, wuk, wuv, cos, sina, sinb)


def _lane_fold(x, op):
    out = x[:, :LANES]
    for c in range(1, x.shape[1] // LANES):
        out = op(out, x[:, c * LANES:(c + 1) * LANES])
    return out


def _attn_reset(mrun_ref, acc_ref):
    mrun_ref[...] = jnp.full_like(mrun_ref, NEG_INF)
    acc_ref[...] = jnp.zeros_like(acc_ref)


def _score_tile(j, s, s_ref, mrun_ref):
    s_ref[j] = s
    mrun_ref[...] = jnp.maximum(mrun_ref[...], _lane_fold(s, jnp.maximum))


def _rowmax_to_lanes(mrun_ref):
    m = jnp.max(mrun_ref[...], axis=1, keepdims=True)
    mrun_ref[...] = jnp.broadcast_to(m, mrun_ref.shape)


def _prob_tile(j, v, s_ref, mrun_ref, acc_ref):
    s = s_ref[j]
    mb = mrun_ref[...]
    p = jnp.concatenate([jnp.exp2(s[:, c * LANES:(c + 1) * LANES] - mb) for c in range(s.shape[1] // LANES)],
                        axis=1).astype(BF16)
    v_ones = jnp.concatenate([v, jnp.ones((v.shape[0], LANES), BF16)], axis=1)
    acc_ref[...] += jnp.dot(p, v_ones, preferred_element_type=F32)


def _attn_result(acc_ref, hd):
    acc = acc_ref[...]
    return acc[:, :hd] / acc[:, hd:], acc[:, hd:]


def _for_tiles(n, body):
    if isinstance(n, int):
        for j in range(n):
            body(j)
        return

    def pair(t, c):
        body(2 * t)
        body(2 * t + 1)
        return c

    lax.fori_loop(0, n // 2, pair, 0)

    @pl.when(n % 2 == 1)
    def _():
        body(n - 1)


_NT = (((1,), (1,)), ((), ()))


def _mla_attn_kernel(q_ref, k_ref, v_ref, o_ref, s_ref, mrun_ref, acc_ref):
    t = q_ref.shape[0]
    i = pl.program_id(2)
    q = q_ref[...]
    _attn_reset(mrun_ref, acc_ref)

    def scores(j):
        k0 = pl.multiple_of(j * t, t)
        return lax.dot_general(q, k_ref[pl.ds(k0, t), :], _NT, preferred_element_type=F32)

    _for_tiles(i, lambda j: _score_tile(j, scores(j), s_ref, mrun_ref))
    rows = lax.broadcasted_iota(jnp.int32, (t, t), 0)
    cols = lax.broadcasted_iota(jnp.int32, (t, t), 1)
    _score_tile(i, jnp.where(cols <= rows, scores(i), NEG_INF), s_ref, mrun_ref)
    _rowmax_to_lanes(mrun_ref)
    _for_tiles(i + 1, lambda j: _prob_tile(j, v_ref[pl.ds(pl.multiple_of(j * t, t), t), :],
                                           s_ref, mrun_ref, acc_ref))
    o_ref[...] = _attn_result(acc_ref, B_DV)[0].astype(o_ref.dtype)


def _mla_attn(qc, kc, vv, bsz, seq, t):
    n = seq // t
    return pl.pallas_call(
        _mla_attn_kernel,
        grid=(bsz, B_HEADS, n),
        in_specs=[pl.BlockSpec((None, t, 2 * LANES), lambda b, h, i: (b, i, h)),
                  pl.BlockSpec((None, seq, 2 * LANES), lambda b, h, i: (b, 0, h)),
                  pl.BlockSpec((None, seq, B_DV), lambda b, h, i: (b, 0, h))],
        out_specs=pl.BlockSpec((None, t, B_DV), lambda b, h, i: (b, i, h)),
        out_shape=jax.ShapeDtypeStruct((bsz, seq, B_HEADS * B_DV), BF16),
        scratch_shapes=[pltpu.VMEM((n, t, t), F32), pltpu.VMEM((t, LANES), F32),
                        pltpu.VMEM((t, B_DV + LANES), F32)],
        compiler_params=_params(("parallel", "parallel", "arbitrary")),
        name="mla_attn",
    )(qc, kc, vv)


def _gelu_tanh(x):
    return 0.5 * x * (1.0 + jnp.tanh(math.sqrt(2.0 / math.pi) * (x + 0.044715 * (x * x * x))))


def _compress_kernel(seg_ref, pe_ref, w1_ref, w2_ref, o_ref):
    half = seg_ref.shape[1]
    nseg = seg_ref.shape[0]
    seg = seg_ref[...].astype(F32)
    pe = pe_ref[...]
    top = (seg + pe[:, :half]).astype(BF16)
    bot = (seg + pe[:, half:]).astype(BF16)
    u = jnp.dot(top, w1_ref[:half, :], preferred_element_type=F32)
    w = jnp.dot(bot, w1_ref[half:, :], preferred_element_type=F32)
    hid = u + pltpu.roll(w, nseg - 1, 0)
    o_ref[...] = jnp.dot(_gelu_tanh(hid).astype(BF16), w2_ref[...], preferred_element_type=F32).astype(BF16)


def _compress(segs, pe, w1, w2):
    _, bsz, g, nseg, width = segs.shape
    return pl.pallas_call(
        _compress_kernel,
        grid=(2, bsz, g),
        in_specs=[pl.BlockSpec((None, None, None, nseg, width), lambda t, b, gg: (t, b, gg, 0, 0)),
                  pl.BlockSpec((None, 1, 2 * width), lambda t, b, gg: (t, 0, 0)),
                  pl.BlockSpec((None, 2 * width, C_CMP_HIDDEN), lambda t, b, gg: (t, 0, 0)),
                  pl.BlockSpec((None, C_CMP_HIDDEN, C_HD), lambda t, b, gg: (t, 0, 0))],
        out_specs=pl.BlockSpec((None, None, None, nseg, C_HD), lambda t, b, gg: (t, b, gg, 0, 0)),
        out_shape=jax.ShapeDtypeStruct((2, bsz, g, nseg, C_HD), BF16),
        compiler_params=_params(("parallel", "parallel", "parallel")),
        name="nsa_compress",
    )(segs, pe, w1, w2)


def _split2(x):
    hi = x.astype(BF16)
    return hi, (x - hi.astype(F32)).astype(BF16)


def _topk_blocks(imp_t2, i):
    nsel, width = imp_t2.shape
    half_w, half_n = width // 2, nsel // 2
    sblk = lax.broadcasted_iota(jnp.int32, imp_t2.shape, 0)
    upper = lax.broadcasted_iota(jnp.int32, imp_t2.shape, 1) >= half_w
    forced = (sblk == 0) | (sblk == i) | (sblk == i - 1)
    key = lax.bitcast_convert_type(jnp.where(forced, FORCE_SCORE, jnp.abs(imp_t2)), jnp.int32)
    key = jnp.where(sblk <= i, key, -(2 ** 30))
    key_m1 = key - 1
    sblk_adj = sblk - jnp.where(upper, half_n, 0)
    rank = jnp.zeros(imp_t2.shape, jnp.int32)
    for t in range(half_n):
        cand = jnp.where(upper[0:1], key[t + half_n:t + half_n + 1, :], key[t:t + 1, :])
        thr = jnp.where(sblk_adj > t, key_m1, key)
        rank = rank + jnp.where(cand > thr, 1, 0)
    rank = rank + pltpu.roll(rank, half_w, 1)
    return rank < C_N_SEL


def _nsa_branch(q, k_ref, v_ref, n_tiles, k0_fn, bias_ref, s_ref, mrun_ref, acc_ref):
    R, QB = q.shape[0], bias_ref.shape[1]
    _attn_reset(mrun_ref, acc_ref)

    def p1(j):
        s = lax.dot_general(q, k_ref[pl.ds(k0_fn(j), SEL_TILE), :], _NT, preferred_element_type=F32)
        s = (s.reshape(R // QB, QB, SEL_TILE) + bias_ref[j][None]).reshape(R, SEL_TILE)
        _score_tile(j, s, s_ref, mrun_ref)

    _for_tiles(n_tiles, p1)
    _rowmax_to_lanes(mrun_ref)
    _for_tiles(n_tiles, lambda j: _prob_tile(j, v_ref[pl.ds(k0_fn(j), SEL_TILE), :], s_ref, mrun_ref, acc_ref))
    return _attn_result(acc_ref, C_HD)[0]


def _nsa_kernel(q_ref, gate_ref, bg_ref, kcmp_ref, vcmp_ref, ks_ref, vs_ref, kw_ref, vw_ref,
                cext_ref, exp_ref, gsel_ref, o_ref, s_ref, bias_ref, wbias_ref, gexp_ref, mrun_ref, acc_ref,
                out_ref):
    QB, HG, HD = C_Q_BLOCK, C_HG, C_HD
    R = QB * HG
    i = pl.program_id(2)
    t0 = i * QB
    scale = HD ** -0.5 * LOG2E

    q_all = q_ref[...]
    q = jnp.concatenate([q_all[:, h * HD:(h + 1) * HD] for h in range(HG)], axis=0)
    q = (q.astype(F32) * scale).astype(BF16)
    tq = t0 + lax.broadcasted_iota(jnp.int32, (QB, 1), 0)
    lane = lax.broadcasted_iota(jnp.int32, (QB, SEL_TILE), 1)

    g_hi, g_lo = _split2(jax.nn.sigmoid(gate_ref[...].astype(F32) + bg_ref[...]))
    gexp_ref[...] = (jnp.dot(g_hi, gsel_ref[...], preferred_element_type=F32)
                     + jnp.dot(g_lo, gsel_ref[...], preferred_element_type=F32))

    def gate_rows(br):
        return jnp.concatenate([gexp_ref[:, (br * HG + h) * LANES:(br * HG + h + 1) * LANES] for h in range(HG)],
                               axis=0)

    ncmp = kcmp_ref.shape[0]
    vis = lax.broadcasted_iota(jnp.int32, (QB, ncmp), 1) * C_CMP_STRIDE + (C_CMP_LEN - 1) <= tq
    s_c = lax.dot_general(q, kcmp_ref[...], _NT, preferred_element_type=F32).reshape(HG, QB, ncmp)
    s_c = s_c + jnp.where(vis, 0.0, NEG_INF)[None]
    e = jnp.exp2(s_c - jnp.max(s_c, axis=2, keepdims=True)) * jnp.where(vis, 1.0, 0.0)[None]
    e_hi, e_lo = _split2(e.reshape(R, ncmp))
    cext = cext_ref[...]
    r_hi = jnp.dot(e_hi, jnp.concatenate([vcmp_ref[...], cext], axis=1), preferred_element_type=F32)
    r_lo = jnp.dot(e_lo, cext, preferred_element_type=F32)
    l_c = r_hi[:, HD:HD + LANES] + r_lo[:, :LANES]
    inv = 1.0 / jnp.where(l_c > 0.0, l_c, 1.0)
    out_ref[...] = gate_rows(0) * (r_hi[:, :HD] * inv)

    imp = jnp.sum(((r_hi[:, HD + LANES:] + r_lo[:, LANES:]) * inv).reshape(HG, QB, LANES), axis=0)
    imp_t = jnp.concatenate([imp, jnp.zeros((LANES - QB, LANES), F32)], axis=0).T
    nsel = exp_ref.shape[0]
    imp_t = imp_t[:nsel]
    sel = _topk_blocks(imp_t + pltpu.roll(imp_t, QB, 1), i)
    drop = jnp.where(sel[:, :QB], 0.0, -1.0).astype(BF16)
    bias = lax.dot_general(drop, exp_ref[...], (((0,), (0,)), ((), ())), preferred_element_type=F32)
    for j in range(bias_ref.shape[0]):
        bias_ref[j] = bias[:, j * SEL_TILE:(j + 1) * SEL_TILE]
    jd = (t0 + QB - 1) // SEL_TILE
    bias_ref[jd] = jnp.where(jd * SEL_TILE + lane <= tq, bias_ref[jd], NEG_INF)

    o_s = _nsa_branch(q, ks_ref, vs_ref, jd + 1, lambda j: pl.multiple_of(j * SEL_TILE, SEL_TILE),
                      bias_ref, s_ref, mrun_ref, acc_ref)
    out_ref[...] += gate_rows(1) * o_s

    n_win = wbias_ref.shape[0]
    kbase = jnp.maximum(t0 + QB - n_win * SEL_TILE, 0)
    for w in range(n_win):
        kpos = kbase + w * SEL_TILE + lane
        wbias_ref[w] = jnp.where((kpos <= tq) & (kpos > tq - C_WINDOW), 0.0, NEG_INF)
    o_w = _nsa_branch(q, kw_ref, vw_ref, n_win, lambda j: pl.multiple_of(kbase + j * SEL_TILE, QB),
                      wbias_ref, s_ref, mrun_ref, acc_ref)
    out = out_ref[...] + gate_rows(2) * o_w
    o_ref[...] = jnp.concatenate([out[h * QB:(h + 1) * QB] for h in range(HG)], axis=1).astype(o_ref.dtype)


def _nsa_attn(p1, cmp_kv, bg, bsz, seq):
    QB, HD, G = C_Q_BLOCK, C_HD, C_KV_GROUPS
    nq = seq // QB
    ncmp = seq // C_CMP_STRIDE
    nsel = seq // C_SEL_LEN
    n_tiles = seq // SEL_TILE
    n_win = (C_WINDOW + QB + SEL_TILE - 1) // SEL_TILE
    R = QB * C_HG
    qw = C_HG * HD
    n_gate = 3 * C_HG
    assert nsel <= LANES and n_gate <= LANES

    cmp_start = np.arange(ncmp) * C_CMP_STRIDE
    sel_start = np.arange(nsel) * C_SEL_LEN
    ovl = ((cmp_start[:, None] < sel_start[None, :] + C_SEL_LEN)
           & (cmp_start[:, None] + C_CMP_LEN > sel_start[None, :])).astype(np.float32)
    cext = np.concatenate([np.ones((ncmp, LANES), np.float32), ovl, np.zeros((ncmp, LANES - nsel), np.float32)], 1)
    expand = (np.arange(seq)[None, :] // C_SEL_LEN == np.arange(nsel)[:, None]).astype(np.float32) * -NEG_INF
    gsel = (np.arange(n_gate * LANES)[None, :] // LANES == np.arange(LANES)[:, None]).astype(np.float32)
    cext, expand, gsel = (jnp.asarray(a, BF16) for a in (cext, expand, gsel))

    def kv_spec(col0):
        return pl.BlockSpec((None, seq, HD), lambda b, g, i: (b, 0, col0 // HD + g))

    return pl.pallas_call(
        _nsa_kernel,
        grid=(bsz, G, nq),
        in_specs=[
            pl.BlockSpec((None, QB, qw), lambda b, g, i: (b, i, g)),
            pl.BlockSpec((None, QB, LANES), lambda b, g, i: (b, i, P1_G // LANES + g)),
            pl.BlockSpec((None, 1, LANES), lambda b, g, i: (g, 0, 0)),
            pl.BlockSpec((None, None, None, ncmp, HD), lambda b, g, i: (0, b, g, 0, 0)),
            pl.BlockSpec((None, None, None, ncmp, HD), lambda b, g, i: (1, b, g, 0, 0)),
            kv_spec(P1_KS), kv_spec(P1_VS), kv_spec(P1_KW), kv_spec(P1_VW),
            pl.BlockSpec((ncmp, 2 * LANES), lambda b, g, i: (0, 0)),
            pl.BlockSpec((nsel, seq), lambda b, g, i: (0, 0)),
            pl.BlockSpec((LANES, n_gate * LANES), lambda b, g, i: (0, 0)),
        ],
        out_specs=pl.BlockSpec((None, QB, qw), lambda b, g, i: (b, i, g)),
        out_shape=jax.ShapeDtypeStruct((bsz, seq, C_HEADS * HD), BF16),
        scratch_shapes=[pltpu.VMEM((n_tiles, R, SEL_TILE), F32),
                        pltpu.VMEM((n_tiles, QB, SEL_TILE), F32),
                        pltpu.VMEM((n_win, QB, SEL_TILE), F32),
                        pltpu.VMEM((QB, n_gate * LANES), F32),
                        pltpu.VMEM((R, LANES), F32),
                        pltpu.VMEM((R, HD + LANES), F32),
                        pltpu.VMEM((R, HD), F32)],
        compiler_params=_params(("parallel", "parallel", "arbitrary")),
        name="nsa_attn",
    )(p1, p1, bg, cmp_kv, cmp_kv, p1, p1, p1, p1, cext, expand, gsel)


def _split_hi_lo(w):
    hi = w.astype(BF16)
    return hi, (w - hi.astype(F32)).astype(BF16)


def _rope_tables(seq):
    half = B_ROPE // 2
    inv = ROPE_THETA ** (-jnp.arange(half, dtype=F32) / half)
    ang = jnp.arange(seq, dtype=F32)[:, None] * inv[None, :]
    cos, sin = jnp.cos(ang), jnp.sin(ang)
    z = jnp.zeros((seq, LANES - B_ROPE), F32)
    zh = jnp.zeros((seq, half), F32)
    return (jnp.concatenate([cos, cos, z], 1), jnp.concatenate([-sin, zh, z], 1),
            jnp.concatenate([zh, sin, z], 1))


def _ffn(x_bf, wg, wu, wd, layer):
    wd = jnp.pad(wd, ((0, FFN_PAD - FFN_HIDDEN), (0, 0))).astype(BF16)
    h = _swiglu_up(x_bf, wg, wu, layer, FFN_PAD, tm=1024, tn=512)
    return _matmul_kgrid(h, wd, tm=1024, tn=2048, tk=1024, name="ffn_down")


def _mixer_ab(x2, x2_bf, bsz, seq, w_in, b_ig, b_fg, mlstm_norm, q_norm, kv_norm, w_uq, w_ukv, w_o):
    m = bsz * seq
    wq, wk, wv, wig, wfg, wog, wcq, wckv, wkr = jnp.split(w_in, AB_SPLITS, axis=1)
    w_main = jnp.concatenate([wq, wk, wv, wog, wcq, wckv], 1).astype(BF16)
    w_misc = jnp.concatenate([wkr, wig, wfg, jnp.zeros((D_MODEL, LANES - B_ROPE - 2 * A_HEADS), F32)], 1)
    p0 = _matmul(x2_bf, w_main, BF16, tm=1024, tn=768, name="proj_ab")
    misc = _misc_proj(x2, *_split_hi_lo(w_misc), tm=512)

    L = MLSTM_CHUNK
    gates = misc[:, MISC_IG:MISC_IG + 2 * A_HEADS].reshape(bsz, seq, 2, A_HEADS)
    gates_c = jnp.transpose(gates, (0, 3, 1, 2))
    gates_r = jnp.transpose(gates.reshape(bsz, seq // L, L, 2, A_HEADS), (0, 4, 1, 3, 2))
    bias = jnp.stack([b_ig, b_fg], -1).reshape(A_HEADS, 1, 2)
    h_a = _mlstm(p0.reshape(bsz, seq, P0_N), gates_r, gates_c, bias,
                 mlstm_norm.reshape(A_HEADS, 1, A_DV), bsz, seq)

    wuq = jnp.pad(w_uq.reshape(B_Q_LORA, B_HEADS, B_NOPE + B_ROPE),
                  ((0, 0), (0, 0), (0, 2 * LANES - B_NOPE - B_ROPE))).reshape(B_Q_LORA, -1).astype(BF16)
    wukv = w_ukv.reshape(B_KV_LORA, B_HEADS, B_NOPE + B_DV)
    wuk = wukv[:, :, :B_NOPE].reshape(B_KV_LORA, -1).astype(BF16)
    wuv = wukv[:, :, B_NOPE:].reshape(B_KV_LORA, -1).astype(BF16)
    cos, sina, sinb = _rope_tables(seq)
    qc, kc, vv = _mla_prep(p0, misc, q_norm.reshape(1, -1), kv_norm.reshape(1, -1), wuq, wuk, wuv,
                           cos, sina, sinb, seq, tm=512)
    h_b = _mla_attn(qc.reshape(bsz, seq, -1), kc.reshape(bsz, seq, -1), vv.reshape(bsz, seq, -1),
                    bsz, seq, t=512)
    h = jnp.concatenate([h_a, h_b], -1).reshape(m, D_MODEL)
    return _matmul(h, w_o.astype(BF16), F32, tm=1024, tn=1024, name="out_ab")


def _mixer_c(x2_bf, bsz, seq, w_in, b_gate, pe_k, pe_v, w1_k, w2_k, w1_v, w2_v, w_o):
    m = bsz * seq
    G, HG, HD = C_KV_GROUPS, C_HG, C_HD
    wg = w_in[:, C_SPLITS[-1]:].reshape(D_MODEL, 3, G, HG)
    wg = jnp.transpose(wg, (0, 2, 1, 3)).reshape(D_MODEL, G, 3 * HG)
    wg = jnp.pad(wg, ((0, 0), (0, 0), (0, LANES - 3 * HG))).reshape(D_MODEL, G * LANES)
    w_main = jnp.concatenate([w_in[:, :C_SPLITS[-1]], wg], 1).astype(BF16)
    bg = jnp.transpose(b_gate.reshape(3, G, HG), (1, 0, 2)).reshape(G, 1, 3 * HG)
    bg = jnp.pad(bg, ((0, 0), (0, 0), (0, LANES - 3 * HG)))
    p1 = _matmul(x2_bf, w_main, BF16, tm=1024, tn=768, name="proj_c")
    p1 = p1.reshape(bsz, seq, P1_N)

    nseg = seq // C_CMP_STRIDE
    kv = p1[:, :, P1_KC:P1_KS].reshape(bsz, nseg, C_CMP_STRIDE, 2, G, HD)
    segs = jnp.transpose(kv, (3, 0, 4, 1, 2, 5)).reshape(2, bsz, G, nseg, C_CMP_STRIDE * HD)
    pe = jnp.stack([pe_k.reshape(1, -1), pe_v.reshape(1, -1)])
    w1 = jnp.stack([w1_k, w1_v]).astype(BF16)
    w2 = jnp.stack([w2_k, w2_v]).astype(BF16)
    cmp_kv = _compress(segs, pe, w1, w2)

    out = _nsa_attn(p1, cmp_kv, bg, bsz, seq)
    return _matmul(out.reshape(m, D_MODEL), w_o.astype(BF16), F32, tm=1024, tn=1024, name="out_c")


def kernel(x, ab_w_in, ab_b_igate, ab_b_fgate, ab_mlstm_norm, ab_q_norm, ab_kv_norm, ab_w_uq, ab_w_ukv, ab_w_o, c_w_in, c_b_gate, c_pe_k, c_pe_v, c_cmp_w1_k, c_cmp_w2_k, c_cmp_w1_v, c_cmp_w2_v, c_w_o, ffn_w_gate, ffn_w_up, ffn_w_down, ln_mix_g, ln_mix_b, ln_ffn_g, ln_ffn_b):
    bsz, seq, d = x.shape
    m = bsz * seq
    x2 = x.reshape(m, d)
    x2_bf = x2.astype(BF16)
    for layer in range(DEPTH):
        j = layer // 2
        if layer % 2 == 0:
            y = _mixer_ab(x2, x2_bf, bsz, seq, ab_w_in[j], ab_b_igate[j], ab_b_fgate[j], ab_mlstm_norm[j],
                          ab_q_norm[j], ab_kv_norm[j], ab_w_uq[j], ab_w_ukv[j], ab_w_o[j])
        else:
            y = _mixer_c(x2_bf, bsz, seq, c_w_in[j], c_b_gate[j], c_pe_k[j], c_pe_v[j], c_cmp_w1_k[j],
                         c_cmp_w2_k[j], c_cmp_w1_v[j], c_cmp_w2_v[j], c_w_o[j])
        x2, x2_bf = _add_ln(x2, y, ln_mix_g[layer], ln_mix_b[layer])
        y = _ffn(x2_bf, ffn_w_gate, ffn_w_up, ffn_w_down[layer], layer)
        x2, x2_bf = _add_ln(x2, y, ln_ffn_g[layer], ln_ffn_b[layer])
    return x2.reshape(bsz, seq, d)
```

```python
import functools
import math

import numpy as np
import jax
import jax.numpy as jnp
from jax import lax
from jax.experimental import pallas as pl
from jax.experimental.pallas import tpu as pltpu

F32 = jnp.float32
BF16 = jnp.bfloat16

D_MODEL = 4096
DEPTH = 2
DN_ALPHA = (2 * DEPTH) ** 0.25
LN_EPS = 1e-5
RMS_EPS = 1e-6
NEG_INF = -1e30
FORCE_SCORE = 1e9
LOG2E = 1.4426950408889634

A_HEADS, A_DQK, A_DV = 4, 256, 512
A_QK, A_V = A_HEADS * A_DQK, A_HEADS * A_DV
A_GATE_CAP = 15.0
B_HEADS, B_Q_LORA, B_KV_LORA, B_NOPE, B_ROPE, B_DV = 16, 1024, 512, 128, 64, 128
ROPE_THETA = 10000.0
C_HEADS, C_KV_GROUPS, C_HD = 32, 4, 128
C_HG = C_HEADS // C_KV_GROUPS
C_KV = C_KV_GROUPS * C_HD
C_CMP_LEN, C_CMP_STRIDE, C_SEL_LEN, C_N_SEL, C_WINDOW, C_CMP_HIDDEN = 32, 16, 64, 16, 512, 256
C_Q_BLOCK = 64
FFN_HIDDEN = 11008
FFN_PAD = 11264

AB_SPLITS = [int(v) for v in np.cumsum([A_QK, A_QK, A_V, A_HEADS, A_HEADS, A_V, B_Q_LORA, B_KV_LORA])]
C_SPLITS = [int(v) for v in np.cumsum([C_HEADS * C_HD] + [C_KV] * 6)]

LANES = 128
VMEM_LIMIT = 56 * 1024 * 1024

P0_Q, P0_K, P0_V, P0_OG, P0_CQ, P0_CKV, P0_N = 0, 1024, 2048, 4096, 6144, 7168, 7680
MISC_IG, MISC_FG = 64, 68
P1_Q, P1_KC, P1_VC, P1_KS, P1_VS, P1_KW, P1_VW, P1_G, P1_N = 0, 4096, 4608, 5120, 5632, 6144, 6656, 7168, 7680

MLSTM_CHUNK = 256
SEL_TILE = 256
N_SEL_BLOCKS_MAX = 64


def _params(sem):
    return pltpu.CompilerParams(dimension_semantics=sem, vmem_limit_bytes=VMEM_LIMIT)


def _mm_kernel(a_ref, b_ref, o_ref):
    o_ref[...] = jnp.dot(a_ref[...], b_ref[...], preferred_element_type=F32).astype(o_ref.dtype)


def _matmul(a, b, out_dtype, tm, tn, name):
    m, k = a.shape
    n = b.shape[1]
    assert m % tm == 0 and n % tn == 0
    return pl.pallas_call(
        _mm_kernel,
        grid=(m // tm, n // tn),
        in_specs=[pl.BlockSpec((tm, k), lambda i, j: (i, 0)),
                  pl.BlockSpec((k, tn), lambda i, j: (0, j))],
        out_specs=pl.BlockSpec((tm, tn), lambda i, j: (i, j)),
        out_shape=jax.ShapeDtypeStruct((m, n), out_dtype),
        compiler_params=_params(("parallel", "arbitrary")),
        name=name,
    )(a, b)


def _mm_acc_kernel(a_ref, b_ref, o_ref, *, k_valid):
    k = pl.program_id(2)
    tk = b_ref.shape[0]
    row = k * tk + lax.broadcasted_iota(jnp.int32, b_ref.shape, 0)
    b = jnp.where(row < k_valid, b_ref[...], 0.0).astype(BF16)

    @pl.when(k == 0)
    def _():
        o_ref[...] = jnp.dot(a_ref[...], b, preferred_element_type=F32)

    @pl.when(k > 0)
    def _():
        o_ref[...] += jnp.dot(a_ref[...], b, preferred_element_type=F32)


def _matmul_kgrid(a, b, layer, tm, tn, tk, name):
    m, kp = a.shape
    _, k, n = b.shape
    assert m % tm == 0 and n % tn == 0 and kp % tk == 0 and kp - k < tk
    return pl.pallas_call(
        functools.partial(_mm_acc_kernel, k_valid=k),
        grid=(m // tm, n // tn, kp // tk),
        in_specs=[pl.BlockSpec((tm, tk), lambda i, j, kk: (i, kk)),
                  pl.BlockSpec((None, tk, tn), lambda i, j, kk: (layer, kk, j))],
        out_specs=pl.BlockSpec((tm, tn), lambda i, j, kk: (i, j)),
        out_shape=jax.ShapeDtypeStruct((m, n), F32),
        compiler_params=_params(("parallel", "arbitrary", "arbitrary")),
        name=name,
    )(a, b)


def _misc_kernel(x_ref, wh_ref, wl_ref, o_ref):
    x = x_ref[...]
    xh = x.astype(BF16)
    xl = (x - xh.astype(F32)).astype(BF16)
    wh = wh_ref[...]
    acc = jnp.dot(xh, wh, preferred_element_type=F32)
    acc += jnp.dot(xl, wh, preferred_element_type=F32)
    acc += jnp.dot(xh, wl_ref[...], preferred_element_type=F32)
    o_ref[...] = acc


def _misc_proj(x, w_hi, w_lo, tm):
    m, k = x.shape
    n = w_hi.shape[1]
    return pl.pallas_call(
        _misc_kernel,
        grid=(m // tm,),
        in_specs=[pl.BlockSpec((tm, k), lambda i: (i, 0)),
                  pl.BlockSpec((k, n), lambda i: (0, 0)),
                  pl.BlockSpec((k, n), lambda i: (0, 0))],
        out_specs=pl.BlockSpec((tm, n), lambda i: (i, 0)),
        out_shape=jax.ShapeDtypeStruct((m, n), F32),
        compiler_params=_params(("parallel",)),
        name="misc_proj",
    )(x, w_hi, w_lo)


def _swiglu_up_kernel(x_ref, wg_ref, wu_ref, o_ref, *, n_valid):
    x = x_ref[...]
    g = jnp.dot(x, wg_ref[...].astype(BF16), preferred_element_type=F32)
    u = jnp.dot(x, wu_ref[...].astype(BF16), preferred_element_type=F32)
    col = pl.program_id(1) * o_ref.shape[1] + lax.broadcasted_iota(jnp.int32, o_ref.shape, 1)
    o_ref[...] = jnp.where(col < n_valid, g * jax.nn.sigmoid(g) * u, 0.0).astype(o_ref.dtype)


def _swiglu_up(x, wg, wu, layer, n_out, tm, tn):
    m, k = x.shape
    n = wg.shape[2]
    assert n_out % tn == 0 and n_out - n < tn
    return pl.pallas_call(
        functools.partial(_swiglu_up_kernel, n_valid=n),
        grid=(m // tm, n_out // tn),
        in_specs=[pl.BlockSpec((tm, k), lambda i, j: (i, 0), pipeline_mode=pl.Buffered(1)),
                  pl.BlockSpec((None, k, tn), lambda i, j: (layer, 0, j)),
                  pl.BlockSpec((None, k, tn), lambda i, j: (layer, 0, j))],
        out_specs=pl.BlockSpec((tm, tn), lambda i, j: (i, j)),
        out_shape=jax.ShapeDtypeStruct((m, n_out), BF16),
        compiler_params=_params(("parallel", "arbitrary")),
        name="swiglu_up",
    )(x, wg, wu)


def _add_ln_kernel(x_ref, y_ref, g_ref, b_ref, o_ref, ob_ref):
    z = DN_ALPHA * x_ref[...] + y_ref[...]
    mu = jnp.mean(z, axis=-1, keepdims=True)
    zc = z - mu
    var = jnp.mean(zc * zc, axis=-1, keepdims=True)
    out = zc * lax.rsqrt(var + LN_EPS) * g_ref[...] + b_ref[...]
    o_ref[...] = out
    ob_ref[...] = out.astype(BF16)


def _add_ln(x, y, g, b, tm=256):
    m, d = x.shape
    row = pl.BlockSpec((tm, d), lambda i: (i, 0))
    vec = pl.BlockSpec((1, d), lambda i: (0, 0))
    return pl.pallas_call(
        _add_ln_kernel,
        grid=(m // tm,),
        in_specs=[row, row, vec, vec],
        out_specs=[row, row],
        out_shape=[jax.ShapeDtypeStruct((m, d), F32), jax.ShapeDtypeStruct((m, d), BF16)],
        compiler_params=_params(("parallel",)),
        name="add_ln",
    )(x, y, g.reshape(1, d), b.reshape(1, d))


def _soft_cap(z):
    return A_GATE_CAP * jnp.tanh(z / A_GATE_CAP)


def _log_sigmoid(z):
    return jnp.minimum(z, 0.0) - jnp.log1p(jnp.exp(-jnp.abs(z)))


def _mlstm_kernel(q_ref, k_ref, v_ref, og_ref, gr_ref, gc_ref, bias_ref, norm_ref, o_ref,
                  c_ref, n_ref, m_ref):
    L = q_ref.shape[0]
    c = pl.program_id(2)

    @pl.when(c == 0)
    def _():
        c_ref[...] = jnp.zeros_like(c_ref)
        n_ref[...] = jnp.zeros_like(n_ref)
        m_ref[...] = jnp.zeros_like(m_ref)

    bias = bias_ref[...]
    gr = gr_ref[...]
    gc = gc_ref[...]
    li_r = _soft_cap(gr[0:1, :] + bias[:, 0:1])
    lf_r = _log_sigmoid(_soft_cap(gr[1:2, :] + bias[:, 1:2]))
    li_c = _soft_cap(gc[:, 0:1] + bias[:, 0:1])
    lf_c = _log_sigmoid(_soft_cap(gc[:, 1:2] + bias[:, 1:2]))

    t_idx = lax.broadcasted_iota(jnp.int32, (L, L), 0)
    s_idx = lax.broadcasted_iota(jnp.int32, (L, L), 1)
    causal = s_idx <= t_idx
    b_c = jnp.sum(jnp.where(causal, lf_r, 0.0), axis=1, keepdims=True)
    b_r = jnp.sum(jnp.where(t_idx <= s_idx, lf_c, 0.0), axis=0, keepdims=True)
    g_tot = jnp.sum(lf_r, axis=1, keepdims=True)

    m_prev = m_ref[...]
    dmat = jnp.where(causal, b_c - b_r + li_r, NEG_INF)
    m_inter = b_c + m_prev
    m_t = jnp.maximum(m_inter, jnp.max(dmat, axis=1, keepdims=True))
    w_inter = jnp.exp(m_inter - m_t)
    pmat = jnp.exp(dmat - m_t)

    q = q_ref[...]
    k = k_ref[...] * (A_DQK ** -0.5)
    v = v_ref[...]
    qk = lax.dot_general(q, k, (((1,), (1,)), ((), ())), preferred_element_type=F32)
    s = qk * pmat
    num = w_inter * jnp.dot(q, c_ref[...].astype(BF16), preferred_element_type=F32)
    num += jnp.dot(s.astype(BF16), v, preferred_element_type=F32)
    den = w_inter * jnp.sum(q.astype(F32) * n_ref[...], axis=1, keepdims=True)
    den += jnp.sum(s, axis=1, keepdims=True)
    h = num / jnp.maximum(jnp.abs(den), jnp.exp(-m_t))

    hn = h * lax.rsqrt(jnp.mean(h * h, axis=-1, keepdims=True) + RMS_EPS) * norm_ref[...]
    o_ref[...] = (hn * jax.nn.sigmoid(og_ref[...].astype(F32))).astype(o_ref.dtype)

    ws_c = g_tot - b_c + li_c
    ws_r = g_tot - b_r + li_r
    m_new = jnp.maximum(g_tot + m_prev, jnp.max(ws_r, axis=1, keepdims=True))
    decay = jnp.exp(g_tot + m_prev - m_new)
    kw = k.astype(F32) * jnp.exp(ws_c - m_new)
    c_ref[...] = decay * c_ref[...] + lax.dot_general(
        kw.astype(BF16), v, (((0,), (0,)), ((), ())), preferred_element_type=F32)
    n_ref[...] = decay * n_ref[...] + jnp.sum(kw, axis=0, keepdims=True)
    m_ref[...] = m_new


def _mlstm(p0, gates_r, gates_c, bias, norm, bsz, seq):
    L = MLSTM_CHUNK
    nc = seq // L
    qb, kb, vb, ogb = P0_Q // A_DQK, P0_K // A_DQK, P0_V // A_DV, P0_OG // A_DV
    return pl.pallas_call(
        _mlstm_kernel,
        grid=(bsz, A_HEADS, nc),
        in_specs=[
            pl.BlockSpec((None, L, A_DQK), lambda b, h, c: (b, c, qb + h)),
            pl.BlockSpec((None, L, A_DQK), lambda b, h, c: (b, c, kb + h)),
            pl.BlockSpec((None, L, A_DV), lambda b, h, c: (b, c, vb + h)),
            pl.BlockSpec((None, L, A_DV), lambda b, h, c: (b, c, ogb + h)),
            pl.BlockSpec((None, None, None, 2, L), lambda b, h, c: (b, h, c, 0, 0)),
            pl.BlockSpec((None, None, L, 2), lambda b, h, c: (b, h, c, 0)),
            pl.BlockSpec((None, 1, 2), lambda b, h, c: (h, 0, 0)),
            pl.BlockSpec((None, 1, A_DV), lambda b, h, c: (h, 0, 0)),
        ],
        out_specs=pl.BlockSpec((None, L, A_DV), lambda b, h, c: (b, c, h)),
        out_shape=jax.ShapeDtypeStruct((bsz, seq, A_V), BF16),
        scratch_shapes=[pltpu.VMEM((A_DQK, A_DV), F32), pltpu.VMEM((1, A_DQK), F32), pltpu.VMEM((1, 1), F32)],
        compiler_params=_params(("parallel", "parallel", "arbitrary")),
        name="mlstm",
    )(p0, p0, p0, p0, gates_r, gates_c, bias, norm)


def _rms(x, g):
    return x * lax.rsqrt(jnp.mean(x * x, axis=-1, keepdims=True) + RMS_EPS) * g


def _rope_tile(x, cos, sina, sinb):
    return x * cos + pltpu.roll(x, 96, 1) * sina + pltpu.roll(x, 32, 1) * sinb


def _mla_prep_kernel(cq_ref, ckv_ref, kr_ref, qg_ref, kvg_ref, wuq_ref, wuk_ref, wuv_ref,
                     cos_ref, sina_ref, sinb_ref, q_ref, k_ref, v_ref):
    scale = (B_NOPE + B_ROPE) ** -0.5 * LOG2E
    cos, sina, sinb = cos_ref[...], sina_ref[...], sinb_ref[...]
    cqn = _rms(cq_ref[...].astype(F32), qg_ref[...]).astype(BF16)
    qb = jnp.dot(cqn, wuq_ref[...], preferred_element_type=F32)
    ckvn = _rms(ckv_ref[...].astype(F32), kvg_ref[...]).astype(BF16)
    kup = jnp.dot(ckvn, wuk_ref[...], preferred_element_type=F32)
    v_ref[...] = jnp.dot(ckvn, wuv_ref[...], preferred_element_type=F32).astype(BF16)
    krope = _rope_tile(kr_ref[...], cos, sina, sinb).astype(BF16)
    for h in range(B_HEADS):
        c0 = 2 * LANES * h
        q_ref[:, c0:c0 + LANES] = (qb[:, c0:c0 + LANES] * scale).astype(BF16)
        q_ref[:, c0 + LANES:c0 + 2 * LANES] = (
            _rope_tile(qb[:, c0 + LANES:c0 + 2 * LANES], cos, sina, sinb) * scale).astype(BF16)
        k_ref[:, c0:c0 + LANES] = kup[:, LANES * h:LANES * (h + 1)].astype(BF16)
        k_ref[:, c0 + LANES:c0 + 2 * LANES] = krope


def _mla_prep(p0, misc, qg, kvg, wuq, wuk, wuv, cos, sina, sinb, seq, tm):
    m = p0.shape[0]
    nrope = seq // tm
    hq = B_HEADS * 2 * LANES
    hv = B_HEADS * B_DV
    full = lambda shape: pl.BlockSpec(shape, lambda i: (0, 0))
    tab = pl.BlockSpec((tm, LANES), lambda i: (i % nrope, 0))
    return pl.pallas_call(
        _mla_prep_kernel,
        grid=(m // tm,),
        in_specs=[
            pl.BlockSpec((tm, B_Q_LORA), lambda i: (i, P0_CQ // B_Q_LORA)),
            pl.BlockSpec((tm, B_KV_LORA), lambda i: (i, P0_CKV // B_KV_LORA)),
            pl.BlockSpec((tm, LANES), lambda i: (i, 0)),
            full((1, B_Q_LORA)), full((1, B_KV_LORA)),
            full((B_Q_LORA, hq)), full((B_KV_LORA, hv)), full((B_KV_LORA, hv)),
            tab, tab, tab,
        ],
        out_specs=[pl.BlockSpec((tm, hq), lambda i: (i, 0)),
                   pl.BlockSpec((tm, hq), lambda i: (i, 0)),
                   pl.BlockSpec((tm, hv), lambda i: (i, 0))],
        out_shape=[jax.ShapeDtypeStruct((m, hq), BF16), jax.ShapeDtypeStruct((m, hq), BF16),
                   jax.ShapeDtypeStruct((m, hv), BF16)],
        compiler_params=_params(("parallel",)),
        name="mla_prep",
    )(p0, p0, misc, qg, kvg, wuq, wuk, wuv, cos, sina, sinb)


def _lane_fold(x, op):
    out = x[:, :LANES]
    for c in range(1, x.shape[1] // LANES):
        out = op(out, x[:, c * LANES:(c + 1) * LANES])
    return out


def _attn_reset(mrun_ref, acc_ref):
    mrun_ref[...] = jnp.full_like(mrun_ref, NEG_INF)
    acc_ref[...] = jnp.zeros_like(acc_ref)


def _score_tile(j, s, s_ref, mrun_ref):
    s_ref[j] = s
    mrun_ref[...] = jnp.maximum(mrun_ref[...], _lane_fold(s, jnp.maximum))


def _rowmax_to_lanes(mrun_ref):
    m = jnp.max(mrun_ref[...], axis=1, keepdims=True)
    mrun_ref[...] = jnp.broadcast_to(m, mrun_ref.shape)


def _prob_tile(j, v, s_ref, mrun_ref, acc_ref):
    s = s_ref[j]
    mb = mrun_ref[...]
    p = jnp.concatenate([jnp.exp2(s[:, c * LANES:(c + 1) * LANES] - mb) for c in range(s.shape[1] // LANES)],
                        axis=1).astype(BF16)
    v_ones = jnp.concatenate([v, jnp.ones((v.shape[0], LANES), BF16)], axis=1)
    acc_ref[...] += jnp.dot(p, v_ones, preferred_element_type=F32)


def _attn_result(acc_ref, hd):
    acc = acc_ref[...]
    return acc[:, :hd] / acc[:, hd:], acc[:, hd:]


def _for_tiles(n, body):
    if isinstance(n, int):
        for j in range(n):
            body(j)
        return

    def pair(t, c):
        body(2 * t)
        body(2 * t + 1)
        return c

    lax.fori_loop(0, n // 2, pair, 0)

    @pl.when(n % 2 == 1)
    def _():
        body(n - 1)


_NT = (((1,), (1,)), ((), ()))


def _mla_attn_kernel(q_ref, k_ref, v_ref, o_ref, s_ref, mrun_ref, acc_ref):
    t = q_ref.shape[0]
    i = pl.program_id(2)
    q = q_ref[...]
    _attn_reset(mrun_ref, acc_ref)

    def scores(j):
        k0 = pl.multiple_of(j * t, t)
        return lax.dot_general(q, k_ref[pl.ds(k0, t), :], _NT, preferred_element_type=F32)

    _for_tiles(i, lambda j: _score_tile(j, scores(j), s_ref, mrun_ref))
    rows = lax.broadcasted_iota(jnp.int32, (t, t), 0)
    cols = lax.broadcasted_iota(jnp.int32, (t, t), 1)
    _score_tile(i, jnp.where(cols <= rows, scores(i), NEG_INF), s_ref, mrun_ref)
    _rowmax_to_lanes(mrun_ref)
    _for_tiles(i + 1, lambda j: _prob_tile(j, v_ref[pl.ds(pl.multiple_of(j * t, t), t), :],
                                           s_ref, mrun_ref, acc_ref))
    o_ref[...] = _attn_result(acc_ref, B_DV)[0].astype(o_ref.dtype)


def _mla_attn(qc, kc, vv, bsz, seq, t):
    n = seq // t
    return pl.pallas_call(
        _mla_attn_kernel,
        grid=(bsz, B_HEADS, n),
        in_specs=[pl.BlockSpec((None, t, 2 * LANES), lambda b, h, i: (b, i, h)),
                  pl.BlockSpec((None, seq, 2 * LANES), lambda b, h, i: (b, 0, h)),
                  pl.BlockSpec((None, seq, B_DV), lambda b, h, i: (b, 0, h))],
        out_specs=pl.BlockSpec((None, t, B_DV), lambda b, h, i: (b, i, h)),
        out_shape=jax.ShapeDtypeStruct((bsz, seq, B_HEADS * B_DV), BF16),
        scratch_shapes=[pltpu.VMEM((n, t, t), F32), pltpu.VMEM((t, LANES), F32),
                        pltpu.VMEM((t, B_DV + LANES), F32)],
        compiler_params=_params(("parallel", "parallel", "arbitrary")),
        name="mla_attn",
    )(qc, kc, vv)


def _gelu_tanh(x):
    return 0.5 * x * (1.0 + jnp.tanh(math.sqrt(2.0 / math.pi) * (x + 0.044715 * (x * x * x))))


def _compress_kernel(seg_ref, pe_ref, w1_ref, w2_ref, o_ref):
    half = seg_ref.shape[1]
    nseg = seg_ref.shape[0]
    seg = seg_ref[...].astype(F32)
    pe = pe_ref[...]
    top = (seg + pe[:, :half]).astype(BF16)
    bot = (seg + pe[:, half:]).astype(BF16)
    u = jnp.dot(top, w1_ref[:half, :], preferred_element_type=F32)
    w = jnp.dot(bot, w1_ref[half:, :], preferred_element_type=F32)
    hid = u + pltpu.roll(w, nseg - 1, 0)
    o_ref[...] = jnp.dot(_gelu_tanh(hid).astype(BF16), w2_ref[...], preferred_element_type=F32).astype(BF16)


def _compress(segs, pe, w1, w2):
    _, bsz, g, nseg, width = segs.shape
    return pl.pallas_call(
        _compress_kernel,
        grid=(2, bsz, g),
        in_specs=[pl.BlockSpec((None, None, None, nseg, width), lambda t, b, gg: (t, b, gg, 0, 0)),
                  pl.BlockSpec((None, 1, 2 * width), lambda t, b, gg: (t, 0, 0)),
                  pl.BlockSpec((None, 2 * width, C_CMP_HIDDEN), lambda t, b, gg: (t, 0, 0)),
                  pl.BlockSpec((None, C_CMP_HIDDEN, C_HD), lambda t, b, gg: (t, 0, 0))],
        out_specs=pl.BlockSpec((None, None, None, nseg, C_HD), lambda t, b, gg: (t, b, gg, 0, 0)),
        out_shape=jax.ShapeDtypeStruct((2, bsz, g, nseg, C_HD), BF16),
        compiler_params=_params(("parallel", "parallel", "parallel")),
        name="nsa_compress",
    )(segs, pe, w1, w2)


def _split2(x):
    hi = x.astype(BF16)
    return hi, (x - hi.astype(F32)).astype(BF16)


def _topk_blocks(imp_t2, i):
    nsel, width = imp_t2.shape
    half_w, half_n = width // 2, nsel // 2
    sblk = lax.broadcasted_iota(jnp.int32, imp_t2.shape, 0)
    upper = lax.broadcasted_iota(jnp.int32, imp_t2.shape, 1) >= half_w
    forced = (sblk == 0) | (sblk == i) | (sblk == i - 1)
    key = lax.bitcast_convert_type(jnp.where(forced, FORCE_SCORE, jnp.abs(imp_t2)), jnp.int32)
    key = jnp.where(sblk <= i, key, -(2 ** 30))
    key_m1 = key - 1
    sblk_adj = sblk - jnp.where(upper, half_n, 0)
    rank = jnp.zeros(imp_t2.shape, jnp.int32)
    for t in range(half_n):
        cand = jnp.where(upper[0:1], key[t + half_n:t + half_n + 1, :], key[t:t + 1, :])
        thr = jnp.where(sblk_adj > t, key_m1, key)
        rank = rank + jnp.where(cand > thr, 1, 0)
    rank = rank + pltpu.roll(rank, half_w, 1)
    return rank < C_N_SEL


def _nsa_branch(q, k_ref, v_ref, n_tiles, k0_fn, bias_ref, s_ref, mrun_ref, acc_ref):
    R, QB = q.shape[0], bias_ref.shape[1]
    _attn_reset(mrun_ref, acc_ref)

    def p1(j):
        s = lax.dot_general(q, k_ref[pl.ds(k0_fn(j), SEL_TILE), :], _NT, preferred_element_type=F32)
        s = (s.reshape(R // QB, QB, SEL_TILE) + bias_ref[j][None]).reshape(R, SEL_TILE)
        _score_tile(j, s, s_ref, mrun_ref)

    _for_tiles(n_tiles, p1)
    _rowmax_to_lanes(mrun_ref)
    _for_tiles(n_tiles, lambda j: _prob_tile(j, v_ref[pl.ds(k0_fn(j), SEL_TILE), :], s_ref, mrun_ref, acc_ref))
    return _attn_result(acc_ref, C_HD)[0]


def _nsa_kernel(q_ref, gate_ref, bg_ref, kcmp_ref, vcmp_ref, ks_ref, vs_ref, kw_ref, vw_ref,
                cext_ref, exp_ref, gsel_ref, o_ref, s_ref, bias_ref, wbias_ref, gexp_ref, mrun_ref, acc_ref,
                out_ref):
    QB, HG, HD = C_Q_BLOCK, C_HG, C_HD
    R = QB * HG
    i = pl.program_id(2)
    t0 = i * QB
    scale = HD ** -0.5 * LOG2E

    q_all = q_ref[...]
    q = jnp.concatenate([q_all[:, h * HD:(h + 1) * HD] for h in range(HG)], axis=0)
    q = (q.astype(F32) * scale).astype(BF16)
    tq = t0 + lax.broadcasted_iota(jnp.int32, (QB, 1), 0)
    lane = lax.broadcasted_iota(jnp.int32, (QB, SEL_TILE), 1)

    g_hi, g_lo = _split2(jax.nn.sigmoid(gate_ref[...].astype(F32) + bg_ref[...]))
    gexp_ref[...] = (jnp.dot(g_hi, gsel_ref[...], preferred_element_type=F32)
                     + jnp.dot(g_lo, gsel_ref[...], preferred_element_type=F32))

    def gate_rows(br):
        return jnp.concatenate([gexp_ref[:, (br * HG + h) * LANES:(br * HG + h + 1) * LANES] for h in range(HG)],
                               axis=0)

    ncmp = kcmp_ref.shape[0]
    vis = lax.broadcasted_iota(jnp.int32, (QB, ncmp), 1) * C_CMP_STRIDE + (C_CMP_LEN - 1) <= tq
    s_c = lax.dot_general(q, kcmp_ref[...], _NT, preferred_element_type=F32).reshape(HG, QB, ncmp)
    s_c = s_c + jnp.where(vis, 0.0, NEG_INF)[None]
    e = jnp.exp2(s_c - jnp.max(s_c, axis=2, keepdims=True)) * jnp.where(vis, 1.0, 0.0)[None]
    e_hi, e_lo = _split2(e.reshape(R, ncmp))
    cext = cext_ref[...]
    r_hi = jnp.dot(e_hi, jnp.concatenate([vcmp_ref[...], cext], axis=1), preferred_element_type=F32)
    r_lo = jnp.dot(e_lo, cext, preferred_element_type=F32)
    l_c = r_hi[:, HD:HD + LANES] + r_lo[:, :LANES]
    inv = 1.0 / jnp.where(l_c > 0.0, l_c, 1.0)
    out_ref[...] = gate_rows(0) * (r_hi[:, :HD] * inv)

    imp = jnp.sum(((r_hi[:, HD + LANES:] + r_lo[:, LANES:]) * inv).reshape(HG, QB, LANES), axis=0)
    imp_t = jnp.concatenate([imp, jnp.zeros((LANES - QB, LANES), F32)], axis=0).T
    nsel = exp_ref.shape[0]
    imp_t = imp_t[:nsel]
    sel = _topk_blocks(imp_t + pltpu.roll(imp_t, QB, 1), i)
    drop = jnp.where(sel[:, :QB], 0.0, -1.0).astype(BF16)
    bias = lax.dot_general(drop, exp_ref[...], (((0,), (0,)), ((), ())), preferred_element_type=F32)
    for j in range(bias_ref.shape[0]):
        bias_ref[j] = bias[:, j * SEL_TILE:(j + 1) * SEL_TILE]
    jd = (t0 + QB - 1) // SEL_TILE
    bias_ref[jd] = jnp.where(jd * SEL_TILE + lane <= tq, bias_ref[jd], NEG_INF)

    o_s = _nsa_branch(q, ks_ref, vs_ref, jd + 1, lambda j: pl.multiple_of(j * SEL_TILE, SEL_TILE),
                      bias_ref, s_ref, mrun_ref, acc_ref)
    out_ref[...] += gate_rows(1) * o_s

    n_win = wbias_ref.shape[0]
    kbase = jnp.maximum(t0 + QB - n_win * SEL_TILE, 0)
    for w in range(n_win):
        kpos = kbase + w * SEL_TILE + lane
        wbias_ref[w] = jnp.where((kpos <= tq) & (kpos > tq - C_WINDOW), 0.0, NEG_INF)
    o_w = _nsa_branch(q, kw_ref, vw_ref, n_win, lambda j: pl.multiple_of(kbase + j * SEL_TILE, QB),
                      wbias_ref, s_ref, mrun_ref, acc_ref)
    out = out_ref[...] + gate_rows(2) * o_w
    o_ref[...] = jnp.concatenate([out[h * QB:(h + 1) * QB] for h in range(HG)], axis=1).astype(o_ref.dtype)


def _nsa_attn(p1, cmp_kv, bg, bsz, seq):
    QB, HD, G = C_Q_BLOCK, C_HD, C_KV_GROUPS
    nq = seq // QB
    ncmp = seq // C_CMP_STRIDE
    nsel = seq // C_SEL_LEN
    n_tiles = seq // SEL_TILE
    n_win = (C_WINDOW + QB + SEL_TILE - 1) // SEL_TILE
    R = QB * C_HG
    qw = C_HG * HD
    n_gate = 3 * C_HG
    assert nsel <= LANES and n_gate <= LANES

    cmp_start = np.arange(ncmp) * C_CMP_STRIDE
    sel_start = np.arange(nsel) * C_SEL_LEN
    ovl = ((cmp_start[:, None] < sel_start[None, :] + C_SEL_LEN)
           & (cmp_start[:, None] + C_CMP_LEN > sel_start[None, :])).astype(np.float32)
    cext = np.concatenate([np.ones((ncmp, LANES), np.float32), ovl, np.zeros((ncmp, LANES - nsel), np.float32)], 1)
    expand = (np.arange(seq)[None, :] // C_SEL_LEN == np.arange(nsel)[:, None]).astype(np.float32) * -NEG_INF
    gsel = (np.arange(n_gate * LANES)[None, :] // LANES == np.arange(LANES)[:, None]).astype(np.float32)
    cext, expand, gsel = (jnp.asarray(a, BF16) for a in (cext, expand, gsel))

    def kv_spec(col0):
        return pl.BlockSpec((None, seq, HD), lambda b, g, i: (b, 0, col0 // HD + g))

    return pl.pallas_call(
        _nsa_kernel,
        grid=(bsz, G, nq),
        in_specs=[
            pl.BlockSpec((None, QB, qw), lambda b, g, i: (b, i, g)),
            pl.BlockSpec((None, QB, LANES), lambda b, g, i: (b, i, P1_G // LANES + g)),
            pl.BlockSpec((None, 1, LANES), lambda b, g, i: (g, 0, 0)),
            pl.BlockSpec((None, None, None, ncmp, HD), lambda b, g, i: (0, b, g, 0, 0)),
            pl.BlockSpec((None, None, None, ncmp, HD), lambda b, g, i: (1, b, g, 0, 0)),
            kv_spec(P1_KS), kv_spec(P1_VS), kv_spec(P1_KW), kv_spec(P1_VW),
            pl.BlockSpec((ncmp, 2 * LANES), lambda b, g, i: (0, 0)),
            pl.BlockSpec((nsel, seq), lambda b, g, i: (0, 0)),
            pl.BlockSpec((LANES, n_gate * LANES), lambda b, g, i: (0, 0)),
        ],
        out_specs=pl.BlockSpec((None, QB, qw), lambda b, g, i: (b, i, g)),
        out_shape=jax.ShapeDtypeStruct((bsz, seq, C_HEADS * HD), BF16),
        scratch_shapes=[pltpu.VMEM((n_tiles, R, SEL_TILE), F32),
                        pltpu.VMEM((n_tiles, QB, SEL_TILE), F32),
                        pltpu.VMEM((n_win, QB, SEL_TILE), F32),
                        pltpu.VMEM((QB, n_gate * LANES), F32),
                        pltpu.VMEM((R, LANES), F32),
                        pltpu.VMEM((R, HD + LANES), F32),
                        pltpu.VMEM((R, HD), F32)],
        compiler_params=_params(("parallel", "parallel", "arbitrary")),
        name="nsa_attn",
    )(p1, p1, bg, cmp_kv, cmp_kv, p1, p1, p1, p1, cext, expand, gsel)


def _split_hi_lo(w):
    hi = w.astype(BF16)
    return hi, (w - hi.astype(F32)).astype(BF16)


def _rope_tables(seq):
    half = B_ROPE // 2
    inv = ROPE_THETA ** (-jnp.arange(half, dtype=F32) / half)
    ang = jnp.arange(seq, dtype=F32)[:, None] * inv[None, :]
    cos, sin = jnp.cos(ang), jnp.sin(ang)
    z = jnp.zeros((seq, LANES - B_ROPE), F32)
    zh = jnp.zeros((seq, half), F32)
    return (jnp.concatenate([cos, cos, z], 1), jnp.concatenate([-sin, zh, z], 1),
            jnp.concatenate([zh, sin, z], 1))


def _ffn(x_bf, wg, wu, wd, layer):
    h = _swiglu_up(x_bf, wg, wu, layer, FFN_PAD, tm=1024, tn=512)
    return _matmul_kgrid(h, wd, layer, tm=1024, tn=2048, tk=1024, name="ffn_down")


def _mixer_ab(x2, x2_bf, bsz, seq, w_in, b_ig, b_fg, mlstm_norm, q_norm, kv_norm, w_uq, w_ukv, w_o):
    m = bsz * seq
    wq, wk, wv, wig, wfg, wog, wcq, wckv, wkr = jnp.split(w_in, AB_SPLITS, axis=1)
    w_main = jnp.concatenate([wq, wk, wv, wog, wcq, wckv], 1).astype(BF16)
    w_misc = jnp.concatenate([wkr, wig, wfg, jnp.zeros((D_MODEL, LANES - B_ROPE - 2 * A_HEADS), F32)], 1)
    p0 = _matmul(x2_bf, w_main, BF16, tm=1024, tn=768, name="proj_ab")
    misc = _misc_proj(x2, *_split_hi_lo(w_misc), tm=512)

    L = MLSTM_CHUNK
    gates = misc[:, MISC_IG:MISC_IG + 2 * A_HEADS].reshape(bsz, seq, 2, A_HEADS)
    gates_c = jnp.transpose(gates, (0, 3, 1, 2))
    gates_r = jnp.transpose(gates.reshape(bsz, seq // L, L, 2, A_HEADS), (0, 4, 1, 3, 2))
    bias = jnp.stack([b_ig, b_fg], -1).reshape(A_HEADS, 1, 2)
    h_a = _mlstm(p0.reshape(bsz, seq, P0_N), gates_r, gates_c, bias,
                 mlstm_norm.reshape(A_HEADS, 1, A_DV), bsz, seq)

    wuq = jnp.pad(w_uq.reshape(B_Q_LORA, B_HEADS, B_NOPE + B_ROPE),
                  ((0, 0), (0, 0), (0, 2 * LANES - B_NOPE - B_ROPE))).reshape(B_Q_LORA, -1).astype(BF16)
    wukv = w_ukv.reshape(B_KV_LORA, B_HEADS, B_NOPE + B_DV)
    wuk = wukv[:, :, :B_NOPE].reshape(B_KV_LORA, -1).astype(BF16)
    wuv = wukv[:, :, B_NOPE:].reshape(B_KV_LORA, -1).astype(BF16)
    cos, sina, sinb = _rope_tables(seq)
    qc, kc, vv = _mla_prep(p0, misc, q_norm.reshape(1, -1), kv_norm.reshape(1, -1), wuq, wuk, wuv,
                           cos, sina, sinb, seq, tm=512)
    h_b = _mla_attn(qc.reshape(bsz, seq, -1), kc.reshape(bsz, seq, -1), vv.reshape(bsz, seq, -1),
                    bsz, seq, t=512)
    h = jnp.concatenate([h_a, h_b], -1).reshape(m, D_MODEL)
    return _matmul(h, w_o.astype(BF16), F32, tm=1024, tn=1024, name="out_ab")


def _mixer_c(x2_bf, bsz, seq, w_in, b_gate, pe_k, pe_v, w1_k, w2_k, w1_v, w2_v, w_o):
    m = bsz * seq
    G, HG, HD = C_KV_GROUPS, C_HG, C_HD
    wg = w_in[:, C_SPLITS[-1]:].reshape(D_MODEL, 3, G, HG)
    wg = jnp.transpose(wg, (0, 2, 1, 3)).reshape(D_MODEL, G, 3 * HG)
    wg = jnp.pad(wg, ((0, 0), (0, 0), (0, LANES - 3 * HG))).reshape(D_MODEL, G * LANES)
    w_main = jnp.concatenate([w_in[:, :C_SPLITS[-1]], wg], 1).astype(BF16)
    bg = jnp.transpose(b_gate.reshape(3, G, HG), (1, 0, 2)).reshape(G, 1, 3 * HG)
    bg = jnp.pad(bg, ((0, 0), (0, 0), (0, LANES - 3 * HG)))
    p1 = _matmul(x2_bf, w_main, BF16, tm=1024, tn=768, name="proj_c")
    p1 = p1.reshape(bsz, seq, P1_N)

    nseg = seq // C_CMP_STRIDE
    kv = p1[:, :, P1_KC:P1_KS].reshape(bsz, nseg, C_CMP_STRIDE, 2, G, HD)
    segs = jnp.transpose(kv, (3, 0, 4, 1, 2, 5)).reshape(2, bsz, G, nseg, C_CMP_STRIDE * HD)
    pe = jnp.stack([pe_k.reshape(1, -1), pe_v.reshape(1, -1)])
    w1 = jnp.stack([w1_k, w1_v]).astype(BF16)
    w2 = jnp.stack([w2_k, w2_v]).astype(BF16)
    cmp_kv = _compress(segs, pe, w1, w2)

    out = _nsa_attn(p1, cmp_kv, bg, bsz, seq)
    return _matmul(out.reshape(m, D_MODEL), w_o.astype(BF16), F32, tm=1024, tn=1024, name="out_c")


def kernel(x, ab_w_in, ab_b_igate, ab_b_fgate, ab_mlstm_norm, ab_q_norm, ab_kv_norm, ab_w_uq, ab_w_ukv, ab_w_o, c_w_in, c_b_gate, c_pe_k, c_pe_v, c_cmp_w1_k, c_cmp_w2_k, c_cmp_w1_v, c_cmp_w2_v, c_w_o, ffn_w_gate, ffn_w_up, ffn_w_down, ln_mix_g, ln_mix_b, ln_ffn_g, ln_ffn_b):
    bsz, seq, d = x.shape
    m = bsz * seq
    x2 = x.reshape(m, d)
    x2_bf = x2.astype(BF16)
    for layer in range(DEPTH):
        j = layer // 2
        if layer % 2 == 0:
            y = _mixer_ab(x2, x2_bf, bsz, seq, ab_w_in[j], ab_b_igate[j], ab_b_fgate[j], ab_mlstm_norm[j],
                          ab_q_norm[j], ab_kv_norm[j], ab_w_uq[j], ab_w_ukv[j], ab_w_o[j])
        else:
            y = _mixer_c(x2_bf, bsz, seq, c_w_in[j], c_b_gate[j], c_pe_k[j], c_pe_v[j], c_cmp_w1_k[j],
                         c_cmp_w2_k[j], c_cmp_w1_v[j], c_cmp_w2_v[j], c_w_o[j])
        x2, x2_bf = _add_ln(x2, y, ln_mix_g[layer], ln_mix_b[layer])
        y = _ffn(x2_bf, ffn_w_gate, ffn_w_up, ffn_w_down, layer)
        x2, x2_bf = _add_ln(x2, y, ln_ffn_g[layer], ln_ffn_b[layer])
    return x2.reshape(bsz, seq, d)
```

```python
import functools
import math

import numpy as np
import jax
import jax.numpy as jnp
from jax import lax
from jax.experimental import pallas as pl
from jax.experimental.pallas import tpu as pltpu

F32 = jnp.float32
BF16 = jnp.bfloat16

D_MODEL = 4096
DEPTH = 2
DN_ALPHA = (2 * DEPTH) ** 0.25
LN_EPS = 1e-5
RMS_EPS = 1e-6
NEG_INF = -1e30
FORCE_SCORE = 1e9
LOG2E = 1.4426950408889634

A_HEADS, A_DQK, A_DV = 4, 256, 512
A_QK, A_V = A_HEADS * A_DQK, A_HEADS * A_DV
A_GATE_CAP = 15.0
B_HEADS, B_Q_LORA, B_KV_LORA, B_NOPE, B_ROPE, B_DV = 16, 1024, 512, 128, 64, 128
ROPE_THETA = 10000.0
C_HEADS, C_KV_GROUPS, C_HD = 32, 4, 128
C_HG = C_HEADS // C_KV_GROUPS
C_KV = C_KV_GROUPS * C_HD
C_CMP_LEN, C_CMP_STRIDE, C_SEL_LEN, C_N_SEL, C_WINDOW, C_CMP_HIDDEN = 32, 16, 64, 16, 512, 256
C_Q_BLOCK = 64
FFN_HIDDEN = 11008
FFN_PAD = 11264

AB_SPLITS = [int(v) for v in np.cumsum([A_QK, A_QK, A_V, A_HEADS, A_HEADS, A_V, B_Q_LORA, B_KV_LORA])]
C_SPLITS = [int(v) for v in np.cumsum([C_HEADS * C_HD] + [C_KV] * 6)]

LANES = 128
VMEM_LIMIT = 56 * 1024 * 1024

P0_Q, P0_K, P0_V, P0_OG, P0_CQ, P0_CKV, P0_N = 0, 1024, 2048, 4096, 6144, 7168, 7680
MISC_IG, MISC_FG = 64, 68
P1_Q, P1_KC, P1_VC, P1_KS, P1_VS, P1_KW, P1_VW, P1_G, P1_N = 0, 4096, 4608, 5120, 5632, 6144, 6656, 7168, 7680

MLSTM_CHUNK = 256
SEL_TILE = 256
N_SEL_BLOCKS_MAX = 64


def _params(sem):
    return pltpu.CompilerParams(dimension_semantics=sem, vmem_limit_bytes=VMEM_LIMIT)


def _mm_kernel(a_ref, b_ref, o_ref):
    o_ref[...] = jnp.dot(a_ref[...], b_ref[...], preferred_element_type=F32).astype(o_ref.dtype)


def _matmul(a, b, out_dtype, tm, tn, name):
    m, k = a.shape
    n = b.shape[1]
    assert m % tm == 0 and n % tn == 0
    return pl.pallas_call(
        _mm_kernel,
        grid=(m // tm, n // tn),
        in_specs=[pl.BlockSpec((tm, k), lambda i, j: (i, 0)),
                  pl.BlockSpec((k, tn), lambda i, j: (0, j))],
        out_specs=pl.BlockSpec((tm, tn), lambda i, j: (i, j)),
        out_shape=jax.ShapeDtypeStruct((m, n), out_dtype),
        compiler_params=_params(("parallel", "arbitrary")),
        name=name,
    )(a, b)


def _mm_cat_kernel(a1_ref, a2_ref, b_ref, o_ref):
    k1 = a1_ref.shape[1]
    acc = jnp.dot(a1_ref[...], b_ref[:k1, :], preferred_element_type=F32)
    o_ref[...] = acc + jnp.dot(a2_ref[...], b_ref[k1:, :], preferred_element_type=F32)


def _matmul_cat(a1, a2, b, tm, tn, name):
    m, k1 = a1.shape
    k2 = a2.shape[1]
    n = b.shape[1]
    assert m % tm == 0 and n % tn == 0 and b.shape[0] == k1 + k2
    return pl.pallas_call(
        _mm_cat_kernel,
        grid=(m // tm, n // tn),
        in_specs=[pl.BlockSpec((tm, k1), lambda i, j: (i, 0)),
                  pl.BlockSpec((tm, k2), lambda i, j: (i, 0)),
                  pl.BlockSpec((k1 + k2, tn), lambda i, j: (0, j))],
        out_specs=pl.BlockSpec((tm, tn), lambda i, j: (i, j)),
        out_shape=jax.ShapeDtypeStruct((m, n), F32),
        compiler_params=_params(("parallel", "arbitrary")),
        name=name,
    )(a1, a2, b)


def _mm_acc_kernel(a_ref, b_ref, o_ref):
    k = pl.program_id(2)

    @pl.when(k == 0)
    def _():
        o_ref[...] = jnp.dot(a_ref[...], b_ref[...], preferred_element_type=F32)

    @pl.when(k > 0)
    def _():
        o_ref[...] += jnp.dot(a_ref[...], b_ref[...], preferred_element_type=F32)


def _matmul_kgrid(a, b, tm, tn, tk, name):
    m, k = a.shape
    n = b.shape[1]
    assert m % tm == 0 and n % tn == 0 and k % tk == 0
    return pl.pallas_call(
        _mm_acc_kernel,
        grid=(m // tm, n // tn, k // tk),
        in_specs=[pl.BlockSpec((tm, tk), lambda i, j, kk: (i, kk)),
                  pl.BlockSpec((tk, tn), lambda i, j, kk: (kk, j))],
        out_specs=pl.BlockSpec((tm, tn), lambda i, j, kk: (i, j)),
        out_shape=jax.ShapeDtypeStruct((m, n), F32),
        compiler_params=_params(("parallel", "arbitrary", "arbitrary")),
        name=name,
    )(a, b)


def _misc_kernel(x_ref, wh_ref, wl_ref, o_ref):
    x = x_ref[...]
    xh = x.astype(BF16)
    xl = (x - xh.astype(F32)).astype(BF16)
    wh = wh_ref[...]
    acc = jnp.dot(xh, wh, preferred_element_type=F32)
    acc += jnp.dot(xl, wh, preferred_element_type=F32)
    acc += jnp.dot(xh, wl_ref[...], preferred_element_type=F32)
    o_ref[...] = acc


def _misc_proj(x, w_hi, w_lo, tm):
    m, k = x.shape
    n = w_hi.shape[1]
    return pl.pallas_call(
        _misc_kernel,
        grid=(m // tm,),
        in_specs=[pl.BlockSpec((tm, k), lambda i: (i, 0)),
                  pl.BlockSpec((k, n), lambda i: (0, 0)),
                  pl.BlockSpec((k, n), lambda i: (0, 0))],
        out_specs=pl.BlockSpec((tm, n), lambda i: (i, 0)),
        out_shape=jax.ShapeDtypeStruct((m, n), F32),
        compiler_params=_params(("parallel",)),
        name="misc_proj",
    )(x, w_hi, w_lo)


def _swiglu_up_kernel(x_ref, wg_ref, wu_ref, o_ref, *, n_valid):
    x = x_ref[...]
    g = jnp.dot(x, wg_ref[...].astype(BF16), preferred_element_type=F32)
    u = jnp.dot(x, wu_ref[...].astype(BF16), preferred_element_type=F32)
    col = pl.program_id(1) * o_ref.shape[1] + lax.broadcasted_iota(jnp.int32, o_ref.shape, 1)
    o_ref[...] = jnp.where(col < n_valid, g * jax.nn.sigmoid(g) * u, 0.0).astype(o_ref.dtype)


def _swiglu_up(x, wg, wu, layer, n_out, tm, tn):
    m, k = x.shape
    n = wg.shape[2]
    assert n_out % tn == 0 and n_out - n < tn
    return pl.pallas_call(
        functools.partial(_swiglu_up_kernel, n_valid=n),
        grid=(m // tm, n_out // tn),
        in_specs=[pl.BlockSpec((tm, k), lambda i, j: (i, 0), pipeline_mode=pl.Buffered(1)),
                  pl.BlockSpec((None, k, tn), lambda i, j: (layer, 0, j)),
                  pl.BlockSpec((None, k, tn), lambda i, j: (layer, 0, j))],
        out_specs=pl.BlockSpec((tm, tn), lambda i, j: (i, j)),
        out_shape=jax.ShapeDtypeStruct((m, n_out), BF16),
        compiler_params=_params(("parallel", "arbitrary")),
        name="swiglu_up",
    )(x, wg, wu)


def _add_ln_kernel(x_ref, y_ref, g_ref, b_ref, o_ref, ob_ref):
    z = DN_ALPHA * x_ref[...] + y_ref[...]
    mu = jnp.mean(z, axis=-1, keepdims=True)
    zc = z - mu
    var = jnp.mean(zc * zc, axis=-1, keepdims=True)
    out = zc * lax.rsqrt(var + LN_EPS) * g_ref[...] + b_ref[...]
    o_ref[...] = out
    ob_ref[...] = out.astype(BF16)


def _add_ln(x, y, g, b, tm=256):
    m, d = x.shape
    row = pl.BlockSpec((tm, d), lambda i: (i, 0))
    vec = pl.BlockSpec((1, d), lambda i: (0, 0))
    return pl.pallas_call(
        _add_ln_kernel,
        grid=(m // tm,),
        in_specs=[row, row, vec, vec],
        out_specs=[row, row],
        out_shape=[jax.ShapeDtypeStruct((m, d), F32), jax.ShapeDtypeStruct((m, d), BF16)],
        compiler_params=_params(("parallel",)),
        name="add_ln",
    )(x, y, g.reshape(1, d), b.reshape(1, d))


def _soft_cap(z):
    return A_GATE_CAP * jnp.tanh(z / A_GATE_CAP)


def _log_sigmoid(z):
    return jnp.minimum(z, 0.0) - jnp.log1p(jnp.exp(-jnp.abs(z)))


def _mlstm_kernel(q_ref, k_ref, v_ref, og_ref, gr_ref, gc_ref, bias_ref, norm_ref, o_ref,
                  c_ref, n_ref, m_ref):
    L = q_ref.shape[0]
    c = pl.program_id(2)

    @pl.when(c == 0)
    def _():
        c_ref[...] = jnp.zeros_like(c_ref)
        n_ref[...] = jnp.zeros_like(n_ref)
        m_ref[...] = jnp.zeros_like(m_ref)

    bias = bias_ref[...]
    gr = gr_ref[...]
    gc = gc_ref[...]
    li_r = _soft_cap(gr[0:1, :] + bias[:, 0:1])
    lf_r = _log_sigmoid(_soft_cap(gr[1:2, :] + bias[:, 1:2]))
    li_c = _soft_cap(gc[:, 0:1] + bias[:, 0:1])
    lf_c = _log_sigmoid(_soft_cap(gc[:, 1:2] + bias[:, 1:2]))

    t_idx = lax.broadcasted_iota(jnp.int32, (L, L), 0)
    s_idx = lax.broadcasted_iota(jnp.int32, (L, L), 1)
    causal = s_idx <= t_idx
    b_c = jnp.sum(jnp.where(causal, lf_r, 0.0), axis=1, keepdims=True)
    b_r = jnp.sum(jnp.where(t_idx <= s_idx, lf_c, 0.0), axis=0, keepdims=True)
    g_tot = jnp.sum(lf_r, axis=1, keepdims=True)

    m_prev = m_ref[...]
    dmat = jnp.where(causal, b_c - b_r + li_r, NEG_INF)
    m_inter = b_c + m_prev
    m_t = jnp.maximum(m_inter, jnp.max(dmat, axis=1, keepdims=True))
    w_inter = jnp.exp(m_inter - m_t)
    pmat = jnp.exp(dmat - m_t)

    q = q_ref[...]
    k = k_ref[...] * (A_DQK ** -0.5)
    v = v_ref[...]
    qk = lax.dot_general(q, k, (((1,), (1,)), ((), ())), preferred_element_type=F32)
    s = qk * pmat
    num = w_inter * jnp.dot(q, c_ref[...].astype(BF16), preferred_element_type=F32)
    num += jnp.dot(s.astype(BF16), v, preferred_element_type=F32)
    den = w_inter * jnp.sum(q.astype(F32) * n_ref[...], axis=1, keepdims=True)
    den += jnp.sum(s, axis=1, keepdims=True)
    h = num / jnp.maximum(jnp.abs(den), jnp.exp(-m_t))

    hn = h * lax.rsqrt(jnp.mean(h * h, axis=-1, keepdims=True) + RMS_EPS) * norm_ref[...]
    o_ref[...] = (hn * jax.nn.sigmoid(og_ref[...].astype(F32))).astype(o_ref.dtype)

    ws_c = g_tot - b_c + li_c
    ws_r = g_tot - b_r + li_r
    m_new = jnp.maximum(g_tot + m_prev, jnp.max(ws_r, axis=1, keepdims=True))
    decay = jnp.exp(g_tot + m_prev - m_new)
    kw = k.astype(F32) * jnp.exp(ws_c - m_new)
    c_ref[...] = decay * c_ref[...] + lax.dot_general(
        kw.astype(BF16), v, (((0,), (0,)), ((), ())), preferred_element_type=F32)
    n_ref[...] = decay * n_ref[...] + jnp.sum(kw, axis=0, keepdims=True)
    m_ref[...] = m_new


def _mlstm(p0, gates_r, gates_c, bias, norm, bsz, seq):
    L = MLSTM_CHUNK
    nc = seq // L
    qb, kb, vb, ogb = P0_Q // A_DQK, P0_K // A_DQK, P0_V // A_DV, P0_OG // A_DV
    return pl.pallas_call(
        _mlstm_kernel,
        grid=(bsz, A_HEADS, nc),
        in_specs=[
            pl.BlockSpec((None, L, A_DQK), lambda b, h, c: (b, c, qb + h)),
            pl.BlockSpec((None, L, A_DQK), lambda b, h, c: (b, c, kb + h)),
            pl.BlockSpec((None, L, A_DV), lambda b, h, c: (b, c, vb + h)),
            pl.BlockSpec((None, L, A_DV), lambda b, h, c: (b, c, ogb + h)),
            pl.BlockSpec((None, None, None, 2, L), lambda b, h, c: (b, h, c, 0, 0)),
            pl.BlockSpec((None, None, L, 2), lambda b, h, c: (b, h, c, 0)),
            pl.BlockSpec((None, 1, 2), lambda b, h, c: (h, 0, 0)),
            pl.BlockSpec((None, 1, A_DV), lambda b, h, c: (h, 0, 0)),
        ],
        out_specs=pl.BlockSpec((None, L, A_DV), lambda b, h, c: (b, c, h)),
        out_shape=jax.ShapeDtypeStruct((bsz, seq, A_V), BF16),
        scratch_shapes=[pltpu.VMEM((A_DQK, A_DV), F32), pltpu.VMEM((1, A_DQK), F32), pltpu.VMEM((1, 1), F32)],
        compiler_params=_params(("parallel", "parallel", "arbitrary")),
        name="mlstm",
    )(p0, p0, p0, p0, gates_r, gates_c, bias, norm)


def _rms(x, g):
    return x * lax.rsqrt(jnp.mean(x * x, axis=-1, keepdims=True) + RMS_EPS) * g


def _rope_tile(x, cos, sina, sinb):
    return x * cos + pltpu.roll(x, 96, 1) * sina + pltpu.roll(x, 32, 1) * sinb


def _mla_prep_kernel(cq_ref, ckv_ref, kr_ref, qg_ref, kvg_ref, wuq_ref, wuk_ref, wuv_ref,
                     cos_ref, sina_ref, sinb_ref, q_ref, k_ref, v_ref):
    scale = (B_NOPE + B_ROPE) ** -0.5 * LOG2E
    cos, sina, sinb = cos_ref[...], sina_ref[...], sinb_ref[...]
    cqn = _rms(cq_ref[...].astype(F32), qg_ref[...]).astype(BF16)
    qb = jnp.dot(cqn, wuq_ref[...], preferred_element_type=F32)
    ckvn = _rms(ckv_ref[...].astype(F32), kvg_ref[...]).astype(BF16)
    kup = jnp.dot(ckvn, wuk_ref[...], preferred_element_type=F32)
    v_ref[...] = jnp.dot(ckvn, wuv_ref[...], preferred_element_type=F32).astype(BF16)
    krope = _rope_tile(kr_ref[...], cos, sina, sinb).astype(BF16)
    for h in range(B_HEADS):
        c0 = 2 * LANES * h
        q_ref[:, c0:c0 + LANES] = (qb[:, c0:c0 + LANES] * scale).astype(BF16)
        q_ref[:, c0 + LANES:c0 + 2 * LANES] = (
            _rope_tile(qb[:, c0 + LANES:c0 + 2 * LANES], cos, sina, sinb) * scale).astype(BF16)
        k_ref[:, c0:c0 + LANES] = kup[:, LANES * h:LANES * (h + 1)].astype(BF16)
        k_ref[:, c0 + LANES:c0 + 2 * LANES] = krope


def _mla_prep(p0, misc, qg, kvg, wuq, wuk, wuv, cos, sina, sinb, seq, tm):
    m = p0.shape[0]
    nrope = seq // tm
    hq = B_HEADS * 2 * LANES
    hv = B_HEADS * B_DV
    full = lambda shape: pl.BlockSpec(shape, lambda i: (0, 0))
    tab = pl.BlockSpec((tm, LANES), lambda i: (i % nrope, 0))
    return pl.pallas_call(
        _mla_prep_kernel,
        grid=(m // tm,),
        in_specs=[
            pl.BlockSpec((tm, B_Q_LORA), lambda i: (i, P0_CQ // B_Q_LORA)),
            pl.BlockSpec((tm, B_KV_LORA), lambda i: (i, P0_CKV // B_KV_LORA)),
            pl.BlockSpec((tm, LANES), lambda i: (i, 0)),
            full((1, B_Q_LORA)), full((1, B_KV_LORA)),
            full((B_Q_LORA, hq)), full((B_KV_LORA, hv)), full((B_KV_LORA, hv)),
            tab, tab, tab,
        ],
        out_specs=[pl.BlockSpec((tm, hq), lambda i: (i, 0)),
                   pl.BlockSpec((tm, hq), lambda i: (i, 0)),
                   pl.BlockSpec((tm, hv), lambda i: (i, 0))],
        out_shape=[jax.ShapeDtypeStruct((m, hq), BF16), jax.ShapeDtypeStruct((m, hq), BF16),
                   jax.ShapeDtypeStruct((m, hv), BF16)],
        compiler_params=_params(("parallel",)),
        name="mla_prep",
    )(p0, p0, misc, qg, kvg, wuq, wuk, wuv, cos, sina, sinb)


def _lane_fold(x, op):
    out = x[:, :LANES]
    for c in range(1, x.shape[1] // LANES):
        out = op(out, x[:, c * LANES:(c + 1) * LANES])
    return out


def _attn_reset(mrun_ref, acc_ref):
    mrun_ref[...] = jnp.full_like(mrun_ref, NEG_INF)
    acc_ref[...] = jnp.zeros_like(acc_ref)


def _score_tile(j, s, s_ref, mrun_ref):
    s_ref[j] = s
    mrun_ref[...] = jnp.maximum(mrun_ref[...], _lane_fold(s, jnp.maximum))


def _rowmax_to_lanes(mrun_ref):
    m = jnp.max(mrun_ref[...], axis=1, keepdims=True)
    mrun_ref[...] = jnp.broadcast_to(m, mrun_ref.shape)


def _prob_tile(j, v, s_ref, mrun_ref, acc_ref):
    s = s_ref[j]
    mb = mrun_ref[...]
    p = jnp.concatenate([jnp.exp2(s[:, c * LANES:(c + 1) * LANES] - mb) for c in range(s.shape[1] // LANES)],
                        axis=1).astype(BF16)
    v_ones = jnp.concatenate([v, jnp.ones((v.shape[0], LANES), BF16)], axis=1)
    acc_ref[...] += jnp.dot(p, v_ones, preferred_element_type=F32)


def _attn_result(acc_ref, hd):
    acc = acc_ref[...]
    return acc[:, :hd] / acc[:, hd:], acc[:, hd:]


def _for_tiles(n, body):
    if isinstance(n, int):
        for j in range(n):
            body(j)
        return

    def pair(t, c):
        body(2 * t)
        body(2 * t + 1)
        return c

    lax.fori_loop(0, n // 2, pair, 0)

    @pl.when(n % 2 == 1)
    def _():
        body(n - 1)


_NT = (((1,), (1,)), ((), ()))


def _mla_attn_kernel(q_ref, k_ref, v_ref, o_ref, s_ref, mrun_ref, acc_ref):
    t = q_ref.shape[0]
    i = pl.program_id(2)
    q = q_ref[...]
    _attn_reset(mrun_ref, acc_ref)

    def scores(j):
        k0 = pl.multiple_of(j * t, t)
        return lax.dot_general(q, k_ref[pl.ds(k0, t), :], _NT, preferred_element_type=F32)

    _for_tiles(i, lambda j: _score_tile(j, scores(j), s_ref, mrun_ref))
    rows = lax.broadcasted_iota(jnp.int32, (t, t), 0)
    cols = lax.broadcasted_iota(jnp.int32, (t, t), 1)
    _score_tile(i, jnp.where(cols <= rows, scores(i), NEG_INF), s_ref, mrun_ref)
    _rowmax_to_lanes(mrun_ref)
    _for_tiles(i + 1, lambda j: _prob_tile(j, v_ref[pl.ds(pl.multiple_of(j * t, t), t), :],
                                           s_ref, mrun_ref, acc_ref))
    o_ref[...] = _attn_result(acc_ref, B_DV)[0].astype(o_ref.dtype)


def _mla_attn(qc, kc, vv, bsz, seq, t):
    n = seq // t
    return pl.pallas_call(
        _mla_attn_kernel,
        grid=(bsz, B_HEADS, n),
        in_specs=[pl.BlockSpec((None, t, 2 * LANES), lambda b, h, i: (b, i, h)),
                  pl.BlockSpec((None, seq, 2 * LANES), lambda b, h, i: (b, 0, h)),
                  pl.BlockSpec((None, seq, B_DV), lambda b, h, i: (b, 0, h))],
        out_specs=pl.BlockSpec((None, t, B_DV), lambda b, h, i: (b, i, h)),
        out_shape=jax.ShapeDtypeStruct((bsz, seq, B_HEADS * B_DV), BF16),
        scratch_shapes=[pltpu.VMEM((n, t, t), F32), pltpu.VMEM((t, LANES), F32),
                        pltpu.VMEM((t, B_DV + LANES), F32)],
        compiler_params=_params(("parallel", "parallel", "arbitrary")),
        name="mla_attn",
    )(qc, kc, vv)


def _gelu_tanh(x):
    return 0.5 * x * (1.0 + jnp.tanh(math.sqrt(2.0 / math.pi) * (x + 0.044715 * (x * x * x))))


def _compress_kernel(seg_ref, pe_ref, w1_ref, w2_ref, o_ref):
    half = seg_ref.shape[1]
    nseg = seg_ref.shape[0]
    seg = seg_ref[...].astype(F32)
    pe = pe_ref[...]
    top = (seg + pe[:, :half]).astype(BF16)
    bot = (seg + pe[:, half:]).astype(BF16)
    u = jnp.dot(top, w1_ref[:half, :], preferred_element_type=F32)
    w = jnp.dot(bot, w1_ref[half:, :], preferred_element_type=F32)
    hid = u + pltpu.roll(w, nseg - 1, 0)
    o_ref[...] = jnp.dot(_gelu_tanh(hid).astype(BF16), w2_ref[...], preferred_element_type=F32).astype(BF16)


def _compress(segs, pe, w1, w2):
    _, bsz, g, nseg, width = segs.shape
    return pl.pallas_call(
        _compress_kernel,
        grid=(2, bsz, g),
        in_specs=[pl.BlockSpec((None, None, None, nseg, width), lambda t, b, gg: (t, b, gg, 0, 0)),
                  pl.BlockSpec((None, 1, 2 * width), lambda t, b, gg: (t, 0, 0)),
                  pl.BlockSpec((None, 2 * width, C_CMP_HIDDEN), lambda t, b, gg: (t, 0, 0)),
                  pl.BlockSpec((None, C_CMP_HIDDEN, C_HD), lambda t, b, gg: (t, 0, 0))],
        out_specs=pl.BlockSpec((None, None, None, nseg, C_HD), lambda t, b, gg: (t, b, gg, 0, 0)),
        out_shape=jax.ShapeDtypeStruct((2, bsz, g, nseg, C_HD), BF16),
        compiler_params=_params(("parallel", "parallel", "parallel")),
        name="nsa_compress",
    )(segs, pe, w1, w2)


def _split2(x):
    hi = x.astype(BF16)
    return hi, (x - hi.astype(F32)).astype(BF16)


def _topk_blocks(imp_t2, i):
    nsel, width = imp_t2.shape
    half_w, half_n = width // 2, nsel // 2
    sblk = lax.broadcasted_iota(jnp.int32, imp_t2.shape, 0)
    upper = lax.broadcasted_iota(jnp.int32, imp_t2.shape, 1) >= half_w
    forced = (sblk == 0) | (sblk == i) | (sblk == i - 1)
    key = lax.bitcast_convert_type(jnp.where(forced, FORCE_SCORE, jnp.abs(imp_t2)), jnp.int32)
    key = jnp.where(sblk <= i, key, -(2 ** 30))
    key_m1 = key - 1
    sblk_adj = sblk - jnp.where(upper, half_n, 0)
    rank = jnp.zeros(imp_t2.shape, jnp.int32)
    for t in range(half_n):
        cand = jnp.where(upper[0:1], key[t + half_n:t + half_n + 1, :], key[t:t + 1, :])
        thr = jnp.where(sblk_adj > t, key_m1, key)
        rank = rank + jnp.where(cand > thr, 1, 0)
    rank = rank + pltpu.roll(rank, half_w, 1)
    return rank < C_N_SEL


def _nsa_branch(q, k_ref, v_ref, n_tiles, k0_fn, bias_ref, s_ref, mrun_ref, acc_ref):
    R, QB = q.shape[0], bias_ref.shape[1]
    _attn_reset(mrun_ref, acc_ref)

    def p1(j):
        s = lax.dot_general(q, k_ref[pl.ds(k0_fn(j), SEL_TILE), :], _NT, preferred_element_type=F32)
        s = (s.reshape(R // QB, QB, SEL_TILE) + bias_ref[j][None]).reshape(R, SEL_TILE)
        _score_tile(j, s, s_ref, mrun_ref)

    _for_tiles(n_tiles, p1)
    _rowmax_to_lanes(mrun_ref)
    _for_tiles(n_tiles, lambda j: _prob_tile(j, v_ref[pl.ds(k0_fn(j), SEL_TILE), :], s_ref, mrun_ref, acc_ref))
    return _attn_result(acc_ref, C_HD)[0]


def _nsa_kernel(q_ref, gate_ref, bg_ref, kcmp_ref, vcmp_ref, ks_ref, vs_ref, kw_ref, vw_ref,
                cext_ref, exp_ref, gsel_ref, o_ref, s_ref, bias_ref, wbias_ref, gexp_ref, mrun_ref, acc_ref,
                out_ref):
    QB, HG, HD = C_Q_BLOCK, C_HG, C_HD
    R = QB * HG
    i = pl.program_id(2)
    t0 = i * QB
    scale = HD ** -0.5 * LOG2E

    q_all = q_ref[...]
    q = jnp.concatenate([q_all[:, h * HD:(h + 1) * HD] for h in range(HG)], axis=0)
    q = (q.astype(F32) * scale).astype(BF16)
    tq = t0 + lax.broadcasted_iota(jnp.int32, (QB, 1), 0)
    lane = lax.broadcasted_iota(jnp.int32, (QB, SEL_TILE), 1)

    g_hi, g_lo = _split2(jax.nn.sigmoid(gate_ref[...].astype(F32) + bg_ref[...]))
    gexp_ref[...] = (jnp.dot(g_hi, gsel_ref[...], preferred_element_type=F32)
                     + jnp.dot(g_lo, gsel_ref[...], preferred_element_type=F32))

    def gate_rows(br):
        return jnp.concatenate([gexp_ref[:, (br * HG + h) * LANES:(br * HG + h + 1) * LANES] for h in range(HG)],
                               axis=0)

    ncmp = kcmp_ref.shape[0]
    vis = lax.broadcasted_iota(jnp.int32, (QB, ncmp), 1) * C_CMP_STRIDE + (C_CMP_LEN - 1) <= tq
    s_c = lax.dot_general(q, kcmp_ref[...], _NT, preferred_element_type=F32).reshape(HG, QB, ncmp)
    s_c = s_c + jnp.where(vis, 0.0, NEG_INF)[None]
    e = jnp.exp2(s_c - jnp.max(s_c, axis=2, keepdims=True)) * jnp.where(vis, 1.0, 0.0)[None]
    e_hi, e_lo = _split2(e.reshape(R, ncmp))
    cext = cext_ref[...]
    r_hi = jnp.dot(e_hi, jnp.concatenate([vcmp_ref[...], cext], axis=1), preferred_element_type=F32)
    r_lo = jnp.dot(e_lo, cext, preferred_element_type=F32)
    l_c = r_hi[:, HD:HD + LANES] + r_lo[:, :LANES]
    inv = 1.0 / jnp.where(l_c > 0.0, l_c, 1.0)
    out_ref[...] = gate_rows(0) * (r_hi[:, :HD] * inv)

    imp = jnp.sum(((r_hi[:, HD + LANES:] + r_lo[:, LANES:]) * inv).reshape(HG, QB, LANES), axis=0)
    imp_t = jnp.concatenate([imp, jnp.zeros((LANES - QB, LANES), F32)], axis=0).T
    nsel = exp_ref.shape[0]
    imp_t = imp_t[:nsel]
    sel = _topk_blocks(imp_t + pltpu.roll(imp_t, QB, 1), i)
    drop = jnp.where(sel[:, :QB], 0.0, -1.0).astype(BF16)
    bias = lax.dot_general(drop, exp_ref[...], (((0,), (0,)), ((), ())), preferred_element_type=F32)
    for j in range(bias_ref.shape[0]):
        bias_ref[j] = bias[:, j * SEL_TILE:(j + 1) * SEL_TILE]
    jd = (t0 + QB - 1) // SEL_TILE
    bias_ref[jd] = jnp.where(jd * SEL_TILE + lane <= tq, bias_ref[jd], NEG_INF)

    o_s = _nsa_branch(q, ks_ref, vs_ref, jd + 1, lambda j: pl.multiple_of(j * SEL_TILE, SEL_TILE),
                      bias_ref, s_ref, mrun_ref, acc_ref)
    out_ref[...] += gate_rows(1) * o_s

    n_win = wbias_ref.shape[0]
    kbase = jnp.maximum(t0 + QB - n_win * SEL_TILE, 0)
    for w in range(n_win):
        kpos = kbase + w * SEL_TILE + lane
        wbias_ref[w] = jnp.where((kpos <= tq) & (kpos > tq - C_WINDOW), 0.0, NEG_INF)
    o_w = _nsa_branch(q, kw_ref, vw_ref, n_win, lambda j: pl.multiple_of(kbase + j * SEL_TILE, QB),
                      wbias_ref, s_ref, mrun_ref, acc_ref)
    out = out_ref[...] + gate_rows(2) * o_w
    o_ref[...] = jnp.concatenate([out[h * QB:(h + 1) * QB] for h in range(HG)], axis=1).astype(o_ref.dtype)


def _nsa_attn(p1, cmp_kv, bg, bsz, seq):
    QB, HD, G = C_Q_BLOCK, C_HD, C_KV_GROUPS
    nq = seq // QB
    ncmp = seq // C_CMP_STRIDE
    nsel = seq // C_SEL_LEN
    n_tiles = seq // SEL_TILE
    n_win = (C_WINDOW + QB + SEL_TILE - 1) // SEL_TILE
    R = QB * C_HG
    qw = C_HG * HD
    n_gate = 3 * C_HG
    assert nsel <= LANES and n_gate <= LANES

    cmp_start = np.arange(ncmp) * C_CMP_STRIDE
    sel_start = np.arange(nsel) * C_SEL_LEN
    ovl = ((cmp_start[:, None] < sel_start[None, :] + C_SEL_LEN)
           & (cmp_start[:, None] + C_CMP_LEN > sel_start[None, :])).astype(np.float32)
    cext = np.concatenate([np.ones((ncmp, LANES), np.float32), ovl, np.zeros((ncmp, LANES - nsel), np.float32)], 1)
    expand = (np.arange(seq)[None, :] // C_SEL_LEN == np.arange(nsel)[:, None]).astype(np.float32) * -NEG_INF
    gsel = (np.arange(n_gate * LANES)[None, :] // LANES == np.arange(LANES)[:, None]).astype(np.float32)
    cext, expand, gsel = (jnp.asarray(a, BF16) for a in (cext, expand, gsel))

    def kv_spec(col0):
        return pl.BlockSpec((None, seq, HD), lambda b, g, i: (b, 0, col0 // HD + g))

    return pl.pallas_call(
        _nsa_kernel,
        grid=(bsz, G, nq),
        in_specs=[
            pl.BlockSpec((None, QB, qw), lambda b, g, i: (b, i, g)),
            pl.BlockSpec((None, QB, LANES), lambda b, g, i: (b, i, P1_G // LANES + g)),
            pl.BlockSpec((None, 1, LANES), lambda b, g, i: (g, 0, 0)),
            pl.BlockSpec((None, None, None, ncmp, HD), lambda b, g, i: (0, b, g, 0, 0)),
            pl.BlockSpec((None, None, None, ncmp, HD), lambda b, g, i: (1, b, g, 0, 0)),
            kv_spec(P1_KS), kv_spec(P1_VS), kv_spec(P1_KW), kv_spec(P1_VW),
            pl.BlockSpec((ncmp, 2 * LANES), lambda b, g, i: (0, 0)),
            pl.BlockSpec((nsel, seq), lambda b, g, i: (0, 0)),
            pl.BlockSpec((LANES, n_gate * LANES), lambda b, g, i: (0, 0)),
        ],
        out_specs=pl.BlockSpec((None, QB, qw), lambda b, g, i: (b, i, g)),
        out_shape=jax.ShapeDtypeStruct((bsz, seq, C_HEADS * HD), BF16),
        scratch_shapes=[pltpu.VMEM((n_tiles, R, SEL_TILE), F32),
                        pltpu.VMEM((n_tiles, QB, SEL_TILE), F32),
                        pltpu.VMEM((n_win, QB, SEL_TILE), F32),
                        pltpu.VMEM((QB, n_gate * LANES), F32),
                        pltpu.VMEM((R, LANES), F32),
                        pltpu.VMEM((R, HD + LANES), F32),
                        pltpu.VMEM((R, HD), F32)],
        compiler_params=_params(("parallel", "parallel", "arbitrary")),
        name="nsa_attn",
    )(p1, p1, bg, cmp_kv, cmp_kv, p1, p1, p1, p1, cext, expand, gsel)


def _split_hi_lo(w):
    hi = w.astype(BF16)
    return hi, (w - hi.astype(F32)).astype(BF16)


def _rope_tables(seq):
    half = B_ROPE // 2
    inv = ROPE_THETA ** (-jnp.arange(half, dtype=F32) / half)
    ang = jnp.arange(seq, dtype=F32)[:, None] * inv[None, :]
    cos, sin = jnp.cos(ang), jnp.sin(ang)
    z = jnp.zeros((seq, LANES - B_ROPE), F32)
    zh = jnp.zeros((seq, half), F32)
    return (jnp.concatenate([cos, cos, z], 1), jnp.concatenate([-sin, zh, z], 1),
            jnp.concatenate([zh, sin, z], 1))


def _ffn(x_bf, wg, wu, wd, layer):
    wd = jnp.pad(wd, ((0, FFN_PAD - FFN_HIDDEN), (0, 0))).astype(BF16)
    h = _swiglu_up(x_bf, wg, wu, layer, FFN_PAD, tm=1024, tn=512)
    return _matmul_kgrid(h, wd, tm=1024, tn=2048, tk=1024, name="ffn_down")


def _mixer_ab(x2, x2_bf, bsz, seq, w_in, b_ig, b_fg, mlstm_norm, q_norm, kv_norm, w_uq, w_ukv, w_o):
    m = bsz * seq
    wq, wk, wv, wig, wfg, wog, wcq, wckv, wkr = jnp.split(w_in, AB_SPLITS, axis=1)
    w_main = jnp.concatenate([wq, wk, wv, wog, wcq, wckv], 1).astype(BF16)
    w_misc = jnp.concatenate([wkr, wig, wfg, jnp.zeros((D_MODEL, LANES - B_ROPE - 2 * A_HEADS), F32)], 1)
    p0 = _matmul(x2_bf, w_main, BF16, tm=1024, tn=768, name="proj_ab")
    misc = _misc_proj(x2, *_split_hi_lo(w_misc), tm=512)

    L = MLSTM_CHUNK
    gates = misc[:, MISC_IG:MISC_IG + 2 * A_HEADS].reshape(bsz, seq, 2, A_HEADS)
    gates_c = jnp.transpose(gates, (0, 3, 1, 2))
    gates_r = jnp.transpose(gates.reshape(bsz, seq // L, L, 2, A_HEADS), (0, 4, 1, 3, 2))
    bias = jnp.stack([b_ig, b_fg], -1).reshape(A_HEADS, 1, 2)
    h_a = _mlstm(p0.reshape(bsz, seq, P0_N), gates_r, gates_c, bias,
                 mlstm_norm.reshape(A_HEADS, 1, A_DV), bsz, seq)

    wuq = jnp.pad(w_uq.reshape(B_Q_LORA, B_HEADS, B_NOPE + B_ROPE),
                  ((0, 0), (0, 0), (0, 2 * LANES - B_NOPE - B_ROPE))).reshape(B_Q_LORA, -1).astype(BF16)
    wukv = w_ukv.reshape(B_KV_LORA, B_HEADS, B_NOPE + B_DV)
    wuk = wukv[:, :, :B_NOPE].reshape(B_KV_LORA, -1).astype(BF16)
    wuv = wukv[:, :, B_NOPE:].reshape(B_KV_LORA, -1).astype(BF16)
    cos, sina, sinb = _rope_tables(seq)
    qc, kc, vv = _mla_prep(p0, misc, q_norm.reshape(1, -1), kv_norm.reshape(1, -1), wuq, wuk, wuv,
                           cos, sina, sinb, seq, tm=512)
    h_b = _mla_attn(qc.reshape(bsz, seq, -1), kc.reshape(bsz, seq, -1), vv.reshape(bsz, seq, -1),
                    bsz, seq, t=512)
    return _matmul_cat(h_a.reshape(m, A_V), h_b.reshape(m, B_HEADS * B_DV), w_o.astype(BF16),
                       tm=1024, tn=1024, name="out_ab")


def _mixer_c(x2_bf, bsz, seq, w_in, b_gate, pe_k, pe_v, w1_k, w2_k, w1_v, w2_v, w_o):
    m = bsz * seq
    G, HG, HD = C_KV_GROUPS, C_HG, C_HD
    wg = w_in[:, C_SPLITS[-1]:].reshape(D_MODEL, 3, G, HG)
    wg = jnp.transpose(wg, (0, 2, 1, 3)).reshape(D_MODEL, G, 3 * HG)
    wg = jnp.pad(wg, ((0, 0), (0, 0), (0, LANES - 3 * HG))).reshape(D_MODEL, G * LANES)
    w_main = jnp.concatenate([w_in[:, :C_SPLITS[-1]], wg], 1).astype(BF16)
    bg = jnp.transpose(b_gate.reshape(3, G, HG), (1, 0, 2)).reshape(G, 1, 3 * HG)
    bg = jnp.pad(bg, ((0, 0), (0, 0), (0, LANES - 3 * HG)))
    p1 = _matmul(x2_bf, w_main, BF16, tm=1024, tn=768, name="proj_c")
    p1 = p1.reshape(bsz, seq, P1_N)

    nseg = seq // C_CMP_STRIDE
    kv = p1[:, :, P1_KC:P1_KS].reshape(bsz, nseg, C_CMP_STRIDE, 2, G, HD)
    segs = jnp.transpose(kv, (3, 0, 4, 1, 2, 5)).reshape(2, bsz, G, nseg, C_CMP_STRIDE * HD)
    pe = jnp.stack([pe_k.reshape(1, -1), pe_v.reshape(1, -1)])
    w1 = jnp.stack([w1_k, w1_v]).astype(BF16)
    w2 = jnp.stack([w2_k, w2_v]).astype(BF16)
    cmp_kv = _compress(segs, pe, w1, w2)

    out = _nsa_attn(p1, cmp_kv, bg, bsz, seq)
    return _matmul(out.reshape(m, D_MODEL), w_o.astype(BF16), F32, tm=1024, tn=1024, name="out_c")


def kernel(x, ab_w_in, ab_b_igate, ab_b_fgate, ab_mlstm_norm, ab_q_norm, ab_kv_norm, ab_w_uq, ab_w_ukv, ab_w_o, c_w_in, c_b_gate, c_pe_k, c_pe_v, c_cmp_w1_k, c_cmp_w2_k, c_cmp_w1_v, c_cmp_w2_v, c_w_o, ffn_w_gate, ffn_w_up, ffn_w_down, ln_mix_g, ln_mix_b, ln_ffn_g, ln_ffn_b):
    bsz, seq, d = x.shape
    m = bsz * seq
    x2 = x.reshape(m, d)
    x2_bf = x2.astype(BF16)
    for layer in range(DEPTH):
        j = layer // 2
        if layer % 2 == 0:
            y = _mixer_ab(x2, x2_bf, bsz, seq, ab_w_in[j], ab_b_igate[j], ab_b_fgate[j], ab_mlstm_norm[j],
                          ab_q_norm[j], ab_kv_norm[j], ab_w_uq[j], ab_w_ukv[j], ab_w_o[j])
        else:
            y = _mixer_c(x2_bf, bsz, seq, c_w_in[j], c_b_gate[j], c_pe_k[j], c_pe_v[j], c_cmp_w1_k[j],
                         c_cmp_w2_k[j], c_cmp_w1_v[j], c_cmp_w2_v[j], c_w_o[j])
        x2, x2_bf = _add_ln(x2, y, ln_mix_g[layer], ln_mix_b[layer])
        y = _ffn(x2_bf, ffn_w_gate, ffn_w_up, ffn_w_down[layer], layer)
        x2, x2_bf = _add_ln(x2, y, ln_ffn_g[layer], ln_ffn_b[layer])
    return x2.reshape(bsz, seq, d)
```
